```python
import math
import jax, jax.numpy as jnp
from jax import lax
import numpy as np

D_MODEL = 2048
BATCH = 4
SEQ = 8192
DEPTH = 4

F32 = jnp.float32
EPS = 1e-6
N_MIXERS = 3
MLA_HEADS = 16
MLA_Q_RANK = 512
MLA_KV_RANK = 512
MLA_NOPE = 128
MLA_ROPE = 64
MLA_V = 128
ROPE_THETA = 10000.0
Q_BLOCK = 128
HG_HEADS = 16
HG_DK = D_MODEL // HG_HEADS
HG_DV = D_MODEL // HG_HEADS
HG_CHUNK = 64
S5_GROUP = 16
S5_GROUPS = D_MODEL // S5_GROUP
S5_STATE = 64
S5_CHUNK = 128
S5_DT_MIN = 1e-3
S5_DT_MAX = 1e-1
D_FF = 5632
N_EXPERTS = 8
TOP_K = 2
N_MLA = (DEPTH + 2) // 3
N_HG = (DEPTH + 1) // 3
N_S5 = DEPTH // 3
N_DENSE = (DEPTH + 1) // 2
N_MOE = DEPTH // 2

kernel_name = "hybrid_mla_hgrn2_s5_moe_trunk"


def _rmsnorm(x, g):
    xf = x.astype(F32)
    y = xf * lax.rsqrt(jnp.mean(xf * xf, axis=-1, keepdims=True) + EPS)
    return (y * g.astype(F32)).astype(x.dtype)


def _rope(x, pos):
    half = x.shape[-1] // 2
    inv = ROPE_THETA ** (-jnp.arange(half, dtype=F32) / half)
    ang = pos.astype(F32)[:, None, None] * inv
    cos, sin = jnp.cos(ang), jnp.sin(ang)
    x1 = x[..., :half].astype(F32)
    x2 = x[..., half:].astype(F32)
    return jnp.concatenate([x1 * cos - x2 * sin, x2 * cos + x1 * sin], axis=-1).astype(x.dtype)


def _causal_block_attention(q, k, v):
    B, S, H, Dq = q.shape
    nb = S // Q_BLOCK
    scale = Dq ** -0.5
    qb = q.reshape(B, nb, Q_BLOCK, H, Dq).transpose(1, 0, 2, 3, 4)
    kpos = jnp.arange(S)

    def one_block(args):
        blk, qi = args
        s = jnp.einsum('bqhd,bkhd->bhqk', qi, k, preferred_element_type=F32) * scale
        qpos = blk * Q_BLOCK + jnp.arange(Q_BLOCK)
        s = jnp.where(kpos[None, :] <= qpos[:, None], s, -jnp.inf)
        p = jax.nn.softmax(s, axis=-1).astype(v.dtype)
        return jnp.einsum('bhqk,bkhd->bqhd', p, v)

    o = lax.map(one_block, (jnp.arange(nb), qb))
    return o.transpose(1, 0, 2, 3, 4).reshape(B, S, H, v.shape[-1])


def _mla(h, w_in, q_norm, kv_norm, w_uq, w_ukv, w_o):
    B, S, _ = h.shape
    H = MLA_HEADS
    proj = h @ w_in
    c_q, c_kv, k_r = jnp.split(proj, [MLA_Q_RANK, MLA_Q_RANK + MLA_KV_RANK], axis=-1)
    q = (_rmsnorm(c_q, q_norm) @ w_uq).reshape(B, S, H, MLA_NOPE + MLA_ROPE)
    kv = (_rmsnorm(c_kv, kv_norm) @ w_ukv).reshape(B, S, H, MLA_NOPE + MLA_V)
    pos = jnp.arange(S)
    q = jnp.concatenate([q[..., :MLA_NOPE], _rope(q[..., MLA_NOPE:], pos)], axis=-1)
    k_r = _rope(k_r[:, :, None, :], pos)
    k = jnp.concatenate([kv[..., :MLA_NOPE], jnp.broadcast_to(k_r, (B, S, H, MLA_ROPE))], axis=-1)
    v = kv[..., MLA_NOPE:]
    o = _causal_block_attention(q, k, v)
    return o.reshape(B, S, H * MLA_V) @ w_o


def _hgrn2(h, w_in, lb, g_norm, w_o):
    B, S, _ = h.shape
    H, DK, DV, C = HG_HEADS, HG_DK, HG_DV, HG_CHUNK
    nc = S // C
    dq = H * DK
    proj = h @ w_in
    q = proj[..., :dq].astype(F32)
    z = proj[..., dq:2 * dq].astype(F32)
    v = proj[..., 2 * dq:2 * dq + H * DV].astype(F32)
    g = proj[..., 2 * dq + H * DV:]
    lb = lb.astype(F32)
    log_f = jnp.logaddexp(jnp.log(lb), jnp.log1p(-lb) + jax.nn.log_sigmoid(z))
    k = (1.0 - lb) * jax.nn.sigmoid(-z)

    def chunks(t, d):
        return t.reshape(B, nc, C, H, d).transpose(1, 0, 3, 2, 4)

    xs = (chunks(q, DK), chunks(k, DK), chunks(v, DV), chunks(log_f, DK))
    causal = jnp.tril(jnp.ones((C, C), dtype=bool))[None, None, :, :, None]

    def step(state, inp):
        qc, kc, vc, lfc = inp
        b = jnp.cumsum(lfc, axis=2)
        o_inter = jnp.einsum('bhtd,bhdv->bhtv', qc * jnp.exp(b), state)
        decay = jnp.exp(jnp.where(causal, b[:, :, :, None, :] - b[:, :, None, :, :], -jnp.inf))
        scores = jnp.einsum('bhtd,bhsd,bhtsd->bhts', qc, kc, decay)
        o_intra = jnp.einsum('bhts,bhsv->bhtv', scores, vc)
        b_last = b[:, :, -1, :]
        state = state * jnp.exp(b_last)[..., None] + jnp.einsum(
            'bhsd,bhsv->bhdv', kc * jnp.exp(b_last[:, :, None, :] - b), vc)
        return state, o_inter + o_intra

    _, o = lax.scan(step, jnp.zeros((B, H, DK, DV), F32), xs)
    o = o.transpose(1, 0, 3, 2, 4).reshape(B, S, H, DV)
    o = _rmsnorm(o, g_norm) * jax.nn.silu(g.astype(F32).reshape(B, S, H, DV))
    return o.reshape(B, S, H * DV).astype(h.dtype) @ w_o


def _s5(h, w_in, a_re, a_im, b_re, b_im, c_re, c_im, d_skip, log_dt, w_out):
    B, S, D = h.shape
    G, N, P, L = S5_GROUPS, S5_GROUP, S5_STATE, S5_CHUNK
    nc = S // L
    u = (h @ w_in).astype(F32).reshape(B, S, G, N)
    a = lax.complex(a_re.astype(F32), a_im.astype(F32))
    dt_a = a * jnp.exp(log_dt.astype(F32))[:, None]
    a_bar = jnp.exp(dt_a)
    b_bar = ((a_bar - 1.0) / a)[:, :, None] * lax.complex(b_re.astype(F32), b_im.astype(F32))
    c = lax.complex(c_re.astype(F32), c_im.astype(F32))
    powers = jnp.exp(jnp.arange(1, L + 1, dtype=F32)[:, None, None] * dt_a)

    def combine(e1, e2):
        a1, b1 = e1
        a2, b2 = e2
        return a1 * a2, a2 * b1 + b2

    def step(carry, uc):
        bu = jnp.einsum('gpn,blgn->blgp', b_bar, uc.astype(jnp.complex64))
        _, h_loc = lax.associative_scan(combine, (jnp.broadcast_to(a_bar, bu.shape), bu), axis=1)
        hs = h_loc + powers * carry[:, None]
        y = jnp.real(jnp.einsum('gnp,blgp->blgn', c, hs))
        return hs[:, -1], y

    uc = u.reshape(B, nc, L, G, N).transpose(1, 0, 2, 3, 4)
    _, y = lax.scan(step, jnp.zeros((B, G, P), jnp.complex64), uc)
    y = y.transpose(1, 0, 2, 3, 4).reshape(B, S, G, N) + d_skip.astype(F32) * u
    y = jax.nn.gelu(y.reshape(B, S, D)).astype(h.dtype)
    zz = y @ w_out
    return zz[..., :D] * jax.nn.sigmoid(zz[..., D:])


def _swiglu(h, w_gu, w_down):
    gate, up = jnp.split(h @ w_gu, 2, axis=-1)
    return (jax.nn.silu(gate) * up) @ w_down


def _moe(h, w_router, w_gu, w_down):
    B, S, D = h.shape
    t = h.reshape(B * S, D)
    logits = (t @ w_router).astype(F32)
    top_v, top_i = lax.top_k(logits, TOP_K)
    gates = jax.nn.softmax(top_v, axis=-1)
    comb = jnp.sum(jax.nn.one_hot(top_i, N_EXPERTS, dtype=F32) * gates[..., None], axis=1)
    out = jnp.zeros((B * S, D), F32)
    for e in range(N_EXPERTS):
        out = out + comb[:, e:e + 1] * _swiglu(t, w_gu[e], w_down[e]).astype(F32)
    return out.astype(h.dtype).reshape(B, S, D)


def setup_inputs(seed: int = 0) -> dict:
    key = jax.random.key(seed)
    it = iter(list(jax.random.split(key, 32)))

    def nrm(shape, scale):
        return jax.random.normal(next(it), shape, F32) * scale

    def gain(shape):
        return 1.0 + nrm(shape, 0.02)

    D = D_MODEL
    res = (2 * DEPTH) ** -0.5
    mla_in = MLA_Q_RANK + MLA_KV_RANK + MLA_ROPE
    hg_in = 2 * HG_HEADS * HG_DK + 2 * HG_HEADS * HG_DV
    G, N, P = S5_GROUPS, S5_GROUP, S5_STATE
    return {
        "x": nrm((BATCH, SEQ, D), 1.0),
        "ln_mix": gain((DEPTH, D)),
        "ln_ffn": gain((DEPTH, D)),
        "ln_final": gain((D,)),
        "mla_w_in": nrm((N_MLA, D, mla_in), D ** -0.5),
        "mla_q_norm": gain((N_MLA, MLA_Q_RANK)),
        "mla_kv_norm": gain((N_MLA, MLA_KV_RANK)),
        "mla_w_uq": nrm((N_MLA, MLA_Q_RANK, MLA_HEADS * (MLA_NOPE + MLA_ROPE)), MLA_Q_RANK ** -0.5),
        "mla_w_ukv": nrm((N_MLA, MLA_KV_RANK, MLA_HEADS * (MLA_NOPE + MLA_V)), MLA_KV_RANK ** -0.5),
        "mla_w_o": nrm((N_MLA, MLA_HEADS * MLA_V, D), (MLA_HEADS * MLA_V) ** -0.5 * res),
        "hg_w_in": nrm((N_HG, D, hg_in), D ** -0.5),
        "hg_lower_bound": nrm((DEPTH, HG_HEADS * HG_DK), 0.1),
        "hg_g_norm": gain((N_HG, HG_DV)),
        "hg_w_o": nrm((N_HG, HG_HEADS * HG_DV, D), (HG_HEADS * HG_DV) ** -0.5 * res),
        "s5_w_in": nrm((N_S5, D, D), D ** -0.5),
        "s5_a_re": -0.5 + nrm((N_S5, G, P), 0.01),
        "s5_a_im": math.pi * jnp.arange(P, dtype=F32) + nrm((N_S5, G, P), 0.01),
        "s5_b_re": nrm((N_S5, G, P, N), (2 * N) ** -0.5),
        "s5_b_im": nrm((N_S5, G, P, N), (2 * N) ** -0.5),
        "s5_c_re": nrm((N_S5, G, N, P), P ** -0.5),
        "s5_c_im": nrm((N_S5, G, N, P), P ** -0.5),
        "s5_d": nrm((N_S5, G, N), 1.0),
        "s5_log_dt": jax.random.uniform(next(it), (N_S5, G), F32, math.log(S5_DT_MIN), math.log(S5_DT_MAX)),
        "s5_w_out": nrm((N_S5, D, 2 * D), D ** -0.5 * res),
        "ffn_w_gu": nrm((N_DENSE, D, 2 * D_FF), D ** -0.5),
        "ffn_w_down": nrm((N_DENSE, D_FF, D), D_FF ** -0.5 * res),
        "moe_w_router": nrm((N_MOE, D, N_EXPERTS), D ** -0.5),
        "moe_w_gu": nrm((N_MOE, N_EXPERTS, D, 2 * D_FF), D ** -0.5),
        "moe_w_down": nrm((N_MOE, N_EXPERTS, D_FF, D), D_FF ** -0.5 * res),
    }


def reference(x, ln_mix, ln_ffn, ln_final,
              mla_w_in, mla_q_norm, mla_kv_norm, mla_w_uq, mla_w_ukv, mla_w_o,
              hg_w_in, hg_lower_bound, hg_g_norm, hg_w_o,
              s5_w_in, s5_a_re, s5_a_im, s5_b_re, s5_b_im, s5_c_re, s5_c_im, s5_d, s5_log_dt, s5_w_out,
              ffn_w_gu, ffn_w_down,
              moe_w_router, moe_w_gu, moe_w_down):
    lb_w = jax.nn.softmax(hg_lower_bound.astype(F32), axis=0)
    lower_bounds = jnp.cumsum(lb_w, axis=0) - lb_w[0]
    for i in range(DEPTH):
        h = _rmsnorm(x, ln_mix[i])
        m, j = i % N_MIXERS, i // N_MIXERS
        if m == 0:
            y = _mla(h, mla_w_in[j], mla_q_norm[j], mla_kv_norm[j], mla_w_uq[j], mla_w_ukv[j], mla_w_o[j])
        elif m == 1:
            y = _hgrn2(h, hg_w_in[j], lower_bounds[i], hg_g_norm[j], hg_w_o[j])
        else:
            y = _s5(h, s5_w_in[j], s5_a_re[j], s5_a_im[j], s5_b_re[j], s5_b_im[j],
                    s5_c_re[j], s5_c_im[j], s5_d[j], s5_log_dt[j], s5_w_out[j])
        x = x + y.astype(x.dtype)
        h = _rmsnorm(x, ln_ffn[i])
        f = i // 2
        if i % 2 == 0:
            y = _swiglu(h, ffn_w_gu[f], ffn_w_down[f])
        else:
            y = _moe(h, moe_w_router[f], moe_w_gu[f], moe_w_down[f])
        x = x + y.astype(x.dtype)
    return _rmsnorm(x, ln_final)
```

```python
import functools
import math

import numpy as np
import jax
import jax.numpy as jnp
from jax import lax
from jax.experimental import pallas as pl
from jax.experimental.pallas import tpu as pltpu

F32 = jnp.float32
BF16 = jnp.bfloat16
EPS = 1e-6
N_MIXERS = 3

MLA_HEADS = 16
MLA_Q_RANK = 512
MLA_KV_RANK = 512
MLA_NOPE = 128
MLA_ROPE = 64
MLA_V = 128
ROPE_THETA = 10000.0
HG_HEADS = 16
HG_CHUNK = 64
S5_GROUP = 16
S5_STATE = 64
S5_BLOCK = 32
N_EXPERTS = 8
TOP_K = 2

LANES = 128
VMEM_LIMIT = 56 * 1024 * 1024

TM = 512
TN = 512
TF = 512
ATT_TILE = 1024
HG_TILE = 512
MOE_TM = 512


def _tile(n, t, step=8):
    if n % step:
        return n
    best = step
    for c in range(step, min(n, t) + 1, step):
        if n % c == 0:
            best = c
    return best


def _params(*sem):
    return pltpu.CompilerParams(dimension_semantics=sem, vmem_limit_bytes=VMEM_LIMIT)


def _rms(x, g):
    return x * lax.rsqrt(jnp.mean(x * x, axis=-1, keepdims=True) + EPS) * g


def _gelu_tanh(y):
    c = math.sqrt(2.0 / math.pi)
    return 0.5 * y * (1.0 + jnp.tanh(c * (y + 0.044715 * (y * y * y))))


def _sigmoid(x):
    return 1.0 / (1.0 + jnp.exp(-x))


def _pro_matmul_kernel(x_ref, g_ref, w_ref, o_ref, h_ref, *, prologue):
    @pl.when(pl.program_id(1) == 0)
    def _():
        x = x_ref[...].astype(F32)
        if prologue == "rms":
            x = _rms(x, g_ref[...])
        h_ref[...] = x.astype(BF16)

    o_ref[...] = jnp.dot(h_ref[...], w_ref[...], preferred_element_type=F32).astype(o_ref.dtype)


def _pro_matmul(x, gain, w, *, prologue, out_dtype, x_col_block=0, name):
    m = x.shape[0]
    k, n = w.shape
    tm, tn = _tile(m, TM), _tile(n, TN, LANES)
    return pl.pallas_call(
        functools.partial(_pro_matmul_kernel, prologue=prologue),
        grid=(m // tm, n // tn),
        in_specs=[
            pl.BlockSpec((tm, k), lambda i, j: (i, x_col_block)),
            pl.BlockSpec((1, k), lambda i, j: (0, 0)),
            pl.BlockSpec((k, tn), lambda i, j: (0, j)),
        ],
        out_specs=pl.BlockSpec((tm, tn), lambda i, j: (i, j)),
        out_shape=jax.ShapeDtypeStruct((m, n), out_dtype),
        scratch_shapes=[pltpu.VMEM((tm, k), BF16)],
        compiler_params=_params("parallel", "arbitrary"),
        name=name,
    )(x, gain.reshape(1, k).astype(F32), w)


def _matmul_res_kernel(a_ref, w_ref, r_ref, o_ref):
    o_ref[...] = r_ref[...] + jnp.dot(a_ref[...], w_ref[...], preferred_element_type=F32)


def _matmul_res(a, w, res, *, name):
    m, k = a.shape
    n = w.shape[1]
    tm, tn = _tile(m, TM), _tile(n, TN, LANES)
    return pl.pallas_call(
        _matmul_res_kernel,
        grid=(m // tm, n // tn),
        in_specs=[
            pl.BlockSpec((tm, k), lambda i, j: (i, 0)),
            pl.BlockSpec((k, tn), lambda i, j: (0, j)),
            pl.BlockSpec((tm, tn), lambda i, j: (i, j)),
        ],
        out_specs=pl.BlockSpec((tm, tn), lambda i, j: (i, j)),
        out_shape=jax.ShapeDtypeStruct((m, n), F32),
        compiler_params=_params("parallel", "parallel"),
        name=name,
    )(a, w, res)


def _s5_out_kernel(y_ref, wv_ref, wg_ref, r_ref, o_ref, h_ref):
    @pl.when(pl.program_id(1) == 0)
    def _():
        h_ref[...] = _gelu_tanh(y_ref[...].astype(F32)).astype(BF16)

    h = h_ref[...]
    val = jnp.dot(h, wv_ref[...], preferred_element_type=F32)
    gate = jnp.dot(h, wg_ref[...], preferred_element_type=F32)
    o_ref[...] = r_ref[...] + val * _sigmoid(gate)


def _s5_out(y, w_out, res):
    m, k = y.shape
    n = w_out.shape[1] // 2
    tm, tn = _tile(m, TM), _tile(n, TN, LANES)
    nb = n // tn
    return pl.pallas_call(
        _s5_out_kernel,
        grid=(m // tm, nb),
        in_specs=[
            pl.BlockSpec((tm, k), lambda i, j: (i, 0)),
            pl.BlockSpec((k, tn), lambda i, j: (0, j)),
            pl.BlockSpec((k, tn), lambda i, j: (0, j + nb)),
            pl.BlockSpec((tm, tn), lambda i, j: (i, j)),
        ],
        out_specs=pl.BlockSpec((tm, tn), lambda i, j: (i, j)),
        out_shape=jax.ShapeDtypeStruct((m, n), F32),
        scratch_shapes=[pltpu.VMEM((tm, k), BF16)],
        compiler_params=_params("parallel", "arbitrary"),
        name="s5_out_glu",
    )(y, w_out, w_out, res)


def _swiglu_step(h, wg, wu, wd):
    g = jnp.dot(h, wg, preferred_element_type=F32)
    u = jnp.dot(h, wu, preferred_element_type=F32)
    a = (g * _sigmoid(g) * u).astype(BF16)
    return jnp.dot(a, wd, preferred_element_type=F32)


def _ffn_kernel(x_ref, g_ref, wg_ref, wu_ref, wd_ref, o_ref, h_ref, acc_ref):
    f = pl.program_id(1)

    @pl.when(f == 0)
    def _():
        h_ref[...] = _rms(x_ref[...], g_ref[...]).astype(BF16)
        acc_ref[...] = jnp.zeros_like(acc_ref)

    acc_ref[...] += _swiglu_step(h_ref[...], wg_ref[...], wu_ref[...], wd_ref[...])

    @pl.when(f == pl.num_programs(1) - 1)
    def _():
        o_ref[...] = x_ref[...] + acc_ref[...]


def _ffn(x, gain, w_gu, w_down):
    m, d = x.shape
    ff = w_down.shape[0]
    tm, tf = _tile(m, TM), _tile(ff, TF, LANES)
    nf = ff // tf
    return pl.pallas_call(
        _ffn_kernel,
        grid=(m // tm, nf),
        in_specs=[
            pl.BlockSpec((tm, d), lambda i, f: (i, 0)),
            pl.BlockSpec((1, d), lambda i, f: (0, 0)),
            pl.BlockSpec((d, tf), lambda i, f: (0, f)),
            pl.BlockSpec((d, tf), lambda i, f: (0, f + nf)),
            pl.BlockSpec((tf, d), lambda i, f: (f, 0)),
        ],
        out_specs=pl.BlockSpec((tm, d), lambda i, f: (i, 0)),
        out_shape=jax.ShapeDtypeStruct((m, d), F32),
        scratch_shapes=[pltpu.VMEM((tm, d), BF16), pltpu.VMEM((tm, d), F32)],
        compiler_params=_params("parallel", "arbitrary"),
        name="ffn_swiglu",
    )(x, gain.reshape(1, d).astype(F32), w_gu, w_gu, w_down)


def _rope_table(seq):
    half = MLA_ROPE // 2
    inv = ROPE_THETA ** (-jnp.arange(half, dtype=F32) / half)
    ang = jnp.arange(seq, dtype=F32)[:, None] * inv
    cos, sin = jnp.cos(ang), jnp.sin(ang)
    return jnp.concatenate([cos, cos, sin, sin], axis=-1)


def _rot_cols(w):
    half = w.shape[-1] // 2
    return jnp.concatenate([-w[..., half:], w[..., :half]], axis=-1)


def _rope_slab(slab, cs):
    prod = slab * cs
    return prod + pltpu.roll(prod, MLA_ROPE, axis=1)


def _krope_kernel(p_ref, cs_ref, o_ref):
    r = _rope_slab(p_ref[...].astype(F32), cs_ref[...])
    lane = lax.broadcasted_iota(jnp.int32, r.shape, 1)
    o_ref[...] = jnp.where(lane < MLA_ROPE, r, 0.0).astype(o_ref.dtype)


def _krope(proj, cs, seq, col_block):
    t = proj.shape[0]
    tm = _tile(seq, TM)
    ns = seq // tm
    return pl.pallas_call(
        _krope_kernel,
        grid=(t // tm,),
        in_specs=[
            pl.BlockSpec((tm, LANES), lambda i: (i, col_block)),
            pl.BlockSpec((tm, LANES), lambda i: (i % ns, 0)),
        ],
        out_specs=pl.BlockSpec((tm, LANES), lambda i: (i, 0)),
        out_shape=jax.ShapeDtypeStruct((t, LANES), BF16),
        compiler_params=_params("parallel"),
        name="mla_k_rope",
    )(proj, cs)


def _attn_kernel(qi_ref, ki_ref, q_ref, cs_ref, kn_ref, kr_ref, v_ref, o_ref,
                 qc_ref, m_ref, l_ref, acc_ref):
    s_id = pl.program_id(2)
    qi = qi_ref[s_id]
    ki = ki_ref[s_id]

    @pl.when(ki == 0)
    def _():
        q = q_ref[...]
        qr = _rope_slab(q[:, LANES:].astype(F32), cs_ref[...]).astype(BF16)
        qc_ref[...] = jnp.concatenate([q[:, :LANES], qr], axis=1)
        m_ref[...] = jnp.full_like(m_ref, -jnp.inf)
        l_ref[...] = jnp.zeros_like(l_ref)
        acc_ref[...] = jnp.zeros_like(acc_ref)

    kc = jnp.concatenate([kn_ref[...], kr_ref[...]], axis=1)
    s = lax.dot_general(qc_ref[...], kc, (((1,), (1,)), ((), ())), preferred_element_type=F32)

    def update(sc):
        m_prev = m_ref[...]
        m_new = jnp.maximum(m_prev, jnp.max(sc, axis=-1, keepdims=True))
        alpha = jnp.exp(m_prev - m_new)
        p = jnp.exp(sc - m_new)
        l_ref[...] = alpha * l_ref[...] + jnp.sum(p, axis=-1, keepdims=True)
        acc_ref[...] = alpha * acc_ref[...] + jnp.dot(p.astype(BF16), v_ref[...], preferred_element_type=F32)
        m_ref[...] = m_new

    @pl.when(ki < qi)
    def _():
        update(s)

    @pl.when(ki == qi)
    def _():
        row = lax.broadcasted_iota(jnp.int32, s.shape, 0)
        col = lax.broadcasted_iota(jnp.int32, s.shape, 1)
        update(jnp.where(col <= row, s, -jnp.inf))
        o_ref[...] = (acc_ref[...] / l_ref[...]).astype(o_ref.dtype)


def _attention(q_ext, kv_ext, k_rope, cs, batch, seq):
    t = q_ext.shape[0]
    heads = q_ext.shape[1] // (2 * LANES)
    tq = _tile(seq, ATT_TILE)
    nq = seq // tq
    steps = [(i, j) for i in range(nq) for j in range(i + 1)]
    qi_tab = jnp.asarray([s[0] for s in steps], jnp.int32)
    ki_tab = jnp.asarray([s[1] for s in steps], jnp.int32)
    grid_spec = pltpu.PrefetchScalarGridSpec(
        num_scalar_prefetch=2,
        grid=(batch, heads, len(steps)),
        in_specs=[
            pl.BlockSpec((tq, 2 * LANES), lambda b, h, s, qi, ki: (b * nq + qi[s], h)),
            pl.BlockSpec((tq, LANES), lambda b, h, s, qi, ki: (qi[s], 0)),
            pl.BlockSpec((tq, LANES), lambda b, h, s, qi, ki: (b * nq + ki[s], 2 * h)),
            pl.BlockSpec((tq, LANES), lambda b, h, s, qi, ki: (b * nq + ki[s], 0)),
            pl.BlockSpec((tq, LANES), lambda b, h, s, qi, ki: (b * nq + ki[s], 2 * h + 1)),
        ],
        out_specs=pl.BlockSpec((tq, LANES), lambda b, h, s, qi, ki: (b * nq + qi[s], h)),
        scratch_shapes=[
            pltpu.VMEM((tq, 2 * LANES), BF16),
            pltpu.VMEM((tq, 1), F32),
            pltpu.VMEM((tq, 1), F32),
            pltpu.VMEM((tq, LANES), F32),
        ],
    )
    return pl.pallas_call(
        _attn_kernel,
        grid_spec=grid_spec,
        out_shape=jax.ShapeDtypeStruct((t, heads * LANES), BF16),
        compiler_params=_params("parallel", "parallel", "arbitrary"),
        name="mla_flash_attention",
    )(qi_tab, ki_tab, q_ext, cs, kv_ext, k_rope, kv_ext)


def _mla(x, ln, w_in, q_norm, kv_norm, w_uq, w_ukv, w_o, batch, seq):
    heads = MLA_HEADS
    assert MLA_NOPE == LANES and MLA_V == LANES and 2 * MLA_ROPE == LANES
    assert MLA_Q_RANK == MLA_KV_RANK
    rank = MLA_Q_RANK
    w_kr = w_in[:, 2 * rank:]
    w_in_ext = jnp.concatenate([w_in, _rot_cols(w_kr)], axis=1).astype(BF16)
    proj = _pro_matmul(x, ln, w_in_ext, prologue="rms", out_dtype=F32, name="mla_in_proj")
    scale = (MLA_NOPE + MLA_ROPE) ** -0.5
    wq = w_uq.reshape(rank, heads, MLA_NOPE + MLA_ROPE) * scale
    wq_ext = jnp.concatenate([wq, _rot_cols(wq[..., MLA_NOPE:])], axis=-1)
    wq_ext = wq_ext.reshape(rank, heads * 2 * LANES).astype(BF16)
    q_ext = _pro_matmul(proj, q_norm, wq_ext, prologue="rms", out_dtype=BF16, x_col_block=0, name="mla_q_up")
    kv_ext = _pro_matmul(proj, kv_norm, w_ukv.astype(BF16), prologue="rms", out_dtype=BF16, x_col_block=1,
                         name="mla_kv_up")
    cs = _rope_table(seq)
    k_rope = _krope(proj, cs, seq, col_block=2 * rank // LANES)
    o = _attention(q_ext, kv_ext, k_rope, cs, batch, seq)
    return _matmul_res(o, w_o.astype(BF16), x, name="mla_out_proj")


def _hg_tables(c):
    levels = int(math.log2(c))
    assert 2 ** levels == c
    t = np.arange(c)[:, None]
    u = np.arange(c)[None, :]
    mats = []
    for l in range(1, levels + 1):
        mid = ((t >> l) << l) + (1 << (l - 1))
        upper = t >= mid
        mats.append(np.where(upper, (u >= mid) & (u <= t), (u > t) & (u <= mid - 1)))
    mats.append(u <= t)
    mats.append(u > t)
    return np.concatenate(mats, axis=0).astype(np.float32), levels


def _split3(x):
    hi = x.astype(BF16)
    r = x - hi.astype(F32)
    mid = r.astype(BF16)
    lo = (r - mid.astype(F32)).astype(BF16)
    return hi, mid, lo


def _hgrn2_kernel(q_ref, z_ref, v_ref, g_ref, lb_ref, gn_ref, tab_ref, o_ref, st_ref, *, chunk, levels):
    c = chunk
    n_chunks = q_ref.shape[0] // c

    @pl.when(pl.program_id(2) == 0)
    def _():
        st_ref[...] = jnp.zeros_like(st_ref)

    lb = lb_ref[...]
    log_lb = jnp.log(lb)
    log_1mlb = jnp.log1p(-lb)
    tab = tab_ref[...]
    row = lax.broadcasted_iota(jnp.int32, (c, c), 0)
    col = lax.broadcasted_iota(jnp.int32, (c, c), 1)

    def body(ci, carry):
        sl = pl.ds(pl.multiple_of(ci * c, c), c)
        z = z_ref[sl, :]
        q = q_ref[sl, :].astype(F32)
        v = v_ref[sl, :]
        e = jnp.exp(-jnp.abs(z))
        log_sig = jnp.minimum(z, 0.0) - jnp.log1p(e)
        bterm = log_1mlb + log_sig
        hi_ = jnp.maximum(log_lb, bterm)
        lf = hi_ + jnp.log1p(jnp.exp(-jnp.abs(log_lb - bterm)))
        k = (1.0 - lb) * jnp.where(z >= 0, e, 1.0) / (1.0 + e)
        p0, p1, p2 = _split3(lf)
        sums = (jnp.dot(tab, p0, preferred_element_type=F32) + jnp.dot(tab, p1, preferred_element_type=F32)
                + jnp.dot(tab, p2, preferred_element_type=F32))
        dec = jnp.exp(sums)
        scores = jnp.where(row == col, jnp.sum(q * k, axis=-1, keepdims=True), 0.0)
        for l in range(1, levels + 1):
            d = dec[(l - 1) * c:l * c]
            sl_ = lax.dot_general((q * d).astype(BF16), (k * d).astype(BF16), (((1,), (1,)), ((), ())),
                                  preferred_element_type=F32)
            same = (row >> l) == (col >> l)
            up = ((row >> (l - 1)) & 1) == 1
            low = ((col >> (l - 1)) & 1) == 0
            scores = scores + jnp.where(same & up & low, sl_, 0.0)
        d_pre = dec[levels * c:(levels + 1) * c]
        d_suf = dec[(levels + 1) * c:(levels + 2) * c]
        st = st_ref[...]
        o = jnp.dot(scores.astype(BF16), v, preferred_element_type=F32)
        o = o + lax.dot_general((q * d_pre).astype(BF16), st.astype(BF16), (((1,), (1,)), ((), ())),
                                preferred_element_type=F32)
        upd = lax.dot_general(v, (k * d_suf).astype(BF16), (((0,), (0,)), ((), ())), preferred_element_type=F32)
        st_ref[...] = st * d_pre[c - 1:c, :] + upd
        gt = g_ref[sl, :].astype(F32)
        o_ref[sl, :] = (_rms(o, gn_ref[...]) * (gt * _sigmoid(gt))).astype(o_ref.dtype)
        return carry

    lax.fori_loop(0, n_chunks, body, 0)


def _hgrn2_core(qvg, z, lb, g_norm, batch, seq):
    t = z.shape[0]
    heads = z.shape[1] // LANES
    ts = _tile(seq, HG_TILE)
    ns = seq // ts
    c = min(HG_CHUNK, ts)
    tab, levels = _hg_tables(c)
    tab = jnp.asarray(tab, BF16)
    return pl.pallas_call(
        functools.partial(_hgrn2_kernel, chunk=c, levels=levels),
        grid=(batch, heads, ns),
        in_specs=[
            pl.BlockSpec((ts, LANES), lambda b, h, s: (b * ns + s, h)),
            pl.BlockSpec((ts, LANES), lambda b, h, s: (b * ns + s, h)),
            pl.BlockSpec((ts, LANES), lambda b, h, s: (b * ns + s, heads + h)),
            pl.BlockSpec((ts, LANES), lambda b, h, s: (b * ns + s, 2 * heads + h)),
            pl.BlockSpec((1, LANES), lambda b, h, s: (0, h)),
            pl.BlockSpec((1, LANES), lambda b, h, s: (0, 0)),
            pl.BlockSpec(tab.shape, lambda b, h, s: (0, 0)),
        ],
        out_specs=pl.BlockSpec((ts, LANES), lambda b, h, s: (b * ns + s, h)),
        out_shape=jax.ShapeDtypeStruct((t, heads * LANES), BF16),
        scratch_shapes=[pltpu.VMEM((LANES, LANES), F32)],
        compiler_params=_params("parallel", "parallel", "arbitrary"),
        name="hgrn2_recurrence",
    )(qvg, z, qvg, qvg, lb.reshape(1, -1).astype(F32), g_norm.reshape(1, -1).astype(F32), tab)


def _hgrn2(x, ln, w_in, lb, g_norm, w_o, batch, seq):
    d = x.shape[1]
    assert d == HG_HEADS * LANES
    w_qvg = jnp.concatenate([w_in[:, :d], w_in[:, 2 * d:]], axis=1).astype(BF16)
    w_z = w_in[:, d:2 * d].astype(BF16)
    qvg = _pro_matmul(x, ln, w_qvg, prologue="rms", out_dtype=BF16, name="hg_in_proj_qvg")
    z = _pro_matmul(x, ln, w_z, prologue="rms", out_dtype=F32, name="hg_in_proj_z")
    o = _hgrn2_core(qvg, z, lb, g_norm, batch, seq)
    return _matmul_res(o, w_o.astype(BF16), x, name="hg_out_proj")


def _s5_tables(a_re, a_im, b_re, b_im, c_re, c_im, d_skip, log_dt, nblk):
    L = S5_BLOCK
    G, P = a_re.shape
    N = b_re.shape[-1]
    a = lax.complex(a_re.astype(F32), a_im.astype(F32))
    dt_a = a * jnp.exp(log_dt.astype(F32))[:, None]
    a_bar = jnp.exp(dt_a)
    b_bar = ((a_bar - 1.0) / a)[:, :, None] * lax.complex(b_re.astype(F32), b_im.astype(F32))
    cc = lax.complex(c_re.astype(F32), c_im.astype(F32))
    pw = jnp.exp(jnp.arange(L + 1, dtype=F32)[:, None, None] * dt_a)
    kern = jnp.real(jnp.einsum("gnp,tgp,gpm->gtnm", cc, pw[:L], b_bar))
    kern = kern.at[:, 0].add(jax.vmap(jnp.diag)(d_skip.astype(F32)))
    s_idx = jnp.arange(L)[:, None]
    t_idx = jnp.arange(L)[None, :]
    lag = t_idx - s_idx
    toep = jnp.where((lag >= 0)[None, :, :, None, None], kern[:, jnp.clip(lag, 0, L - 1)], 0.0)
    toep = toep.transpose(0, 1, 4, 2, 3).reshape(G, L * N, L * N)
    wst = jnp.einsum("sgp,gpm->gsmp", pw[:L][::-1], b_bar)
    w_st = jnp.concatenate([jnp.real(wst), jnp.imag(wst)], axis=-1).reshape(G, L * N, 2 * P)
    co = jnp.einsum("gnp,tgp->gptn", cc, pw[1:])
    c_out = jnp.concatenate([jnp.real(co), -jnp.imag(co)], axis=1).reshape(G, 2 * P, L * N)
    nsteps = max(1, int(math.ceil(math.log2(nblk))))
    lam_k = jnp.exp((L * 2.0 ** jnp.arange(nsteps, dtype=F32))[None, :, None] * dt_a[:, None, :])
    lr, li = jnp.real(lam_k), jnp.imag(lam_k)
    lam = jnp.stack([jnp.concatenate([lr, lr], -1), jnp.concatenate([-li, li], -1)], axis=2)
    return toep.astype(BF16), w_st.astype(BF16), c_out.astype(BF16), lam.astype(F32)


def _s5_kernel(u_ref, toep_ref, wst_ref, cout_ref, lam_ref, y_ref, *, nblk, nsteps):
    u = u_ref[0]
    z = jnp.dot(u, wst_ref[0], preferred_element_type=F32)
    rows, p2 = z.shape
    blk = lax.broadcasted_iota(jnp.int32, (rows, p2), 0) % nblk
    h = z
    for k in range(nsteps):
        sh = 1 << k
        prev = jnp.where(blk >= sh, pltpu.roll(h, sh, axis=0), 0.0)
        lam = lam_ref[0, k]
        h = h + prev * lam[0:1, :] + pltpu.roll(prev, p2 // 2, axis=1) * lam[1:2, :]
    h0 = jnp.where(blk >= 1, pltpu.roll(h, 1, axis=0), 0.0)
    y = jnp.dot(u, toep_ref[0], preferred_element_type=F32)
    y = y + jnp.dot(h0.astype(BF16), cout_ref[0], preferred_element_type=F32)
    y_ref[0] = y.astype(y_ref.dtype)


def _s5_core(u_g, toep, w_st, c_out, lam, nblk):
    g, rows, ln = u_g.shape
    p2 = w_st.shape[-1]
    nsteps = lam.shape[1]
    return pl.pallas_call(
        functools.partial(_s5_kernel, nblk=nblk, nsteps=nsteps),
        grid=(g,),
        in_specs=[
            pl.BlockSpec((1, rows, ln), lambda i: (i, 0, 0)),
            pl.BlockSpec((1, ln, ln), lambda i: (i, 0, 0)),
            pl.BlockSpec((1, ln, p2), lambda i: (i, 0, 0)),
            pl.BlockSpec((1, p2, ln), lambda i: (i, 0, 0)),
            pl.BlockSpec((1, nsteps, 2, p2), lambda i: (i, 0, 0, 0)),
        ],
        out_specs=pl.BlockSpec((1, rows, ln), lambda i: (i, 0, 0)),
        out_shape=jax.ShapeDtypeStruct((g, rows, ln), BF16),
        compiler_params=_params("parallel"),
        name="s5_ssm",
    )(u_g, toep, w_st, c_out, lam)


def _s5(x, ln, w_in, a_re, a_im, b_re, b_im, c_re, c_im, d_skip, log_dt, w_out, batch, seq):
    t, d = x.shape
    G, N, L = d // S5_GROUP, S5_GROUP, S5_BLOCK
    nblk = seq // L
    u = _pro_matmul(x, ln, w_in.astype(BF16), prologue="rms", out_dtype=BF16, name="s5_in_proj")
    toep, w_st, c_out, lam = _s5_tables(a_re, a_im, b_re, b_im, c_re, c_im, d_skip, log_dt, nblk)
    u_g = u.reshape(batch, nblk, L, G, N).transpose(3, 0, 1, 2, 4).reshape(G, batch * nblk, L * N)
    y_g = _s5_core(u_g, toep, w_st, c_out, lam, nblk)
    y = y_g.reshape(G, batch, nblk, L, N).transpose(1, 2, 3, 0, 4).reshape(t, d)
    return _s5_out(y, w_out.astype(BF16), x)


def _router_kernel(x_ref, g_ref, w_ref, h_ref, r_ref):
    h = _rms(x_ref[...], g_ref[...])
    h_ref[...] = h
    h0 = h.astype(BF16)
    h1 = (h - h0.astype(F32)).astype(BF16)
    w = w_ref[...]
    w0 = w.astype(BF16)
    w1 = (w - w0.astype(F32)).astype(BF16)
    logits = (jnp.dot(h0, w0, preferred_element_type=F32) + jnp.dot(h0, w1, preferred_element_type=F32)
              + jnp.dot(h1, w0, preferred_element_type=F32))
    lane = lax.broadcasted_iota(jnp.int32, logits.shape, 1).astype(F32)
    neg = -jnp.inf
    lg = jnp.where(lane < N_EXPERTS, logits, neg)
    m1 = jnp.max(lg, axis=-1, keepdims=True)
    i1 = jnp.min(jnp.where(lg == m1, lane, float(LANES)), axis=-1, keepdims=True)
    lg2 = jnp.where(lane == i1, neg, lg)
    m2 = jnp.max(lg2, axis=-1, keepdims=True)
    i2 = jnp.min(jnp.where(lg2 == m2, lane, float(LANES)), axis=-1, keepdims=True)
    e2 = jnp.exp(m2 - m1)
    g1 = 1.0 / (1.0 + e2)
    g2 = e2 / (1.0 + e2)
    out = jnp.where(lane == 0, g1, jnp.where(lane == 1, g2, jnp.where(lane == 2, i1, jnp.where(lane == 3, i2, 0.0))))
    r_ref[...] = out[:, :r_ref.shape[1]]


def _router(x, gain, w_router):
    m, d = x.shape
    tm = _tile(m, TM)
    w_pad = jnp.zeros((d, LANES), F32).at[:, :N_EXPERTS].set(w_router.astype(F32))
    return pl.pallas_call(
        _router_kernel,
        grid=(m // tm,),
        in_specs=[
            pl.BlockSpec((tm, d), lambda i: (i, 0)),
            pl.BlockSpec((1, d), lambda i: (0, 0)),
            pl.BlockSpec((d, LANES), lambda i: (0, 0)),
        ],
        out_specs=[pl.BlockSpec((tm, d), lambda i: (i, 0)), pl.BlockSpec((tm, 8), lambda i: (i, 0))],
        out_shape=[jax.ShapeDtypeStruct((m, d), F32), jax.ShapeDtypeStruct((m, 8), F32)],
        compiler_params=_params("parallel"),
        name="moe_router_top2",
    )(x, gain.reshape(1, d).astype(F32), w_pad)


def _moe_kernel(te_ref, ta_ref, dest_hbm, h_hbm, gate_ref, wg_ref, wu_ref, wd_ref, y_hbm,
                idx_ref, x_ref, xb_ref, acc_ref, sem_i, sem_g, sem_s, *, n_tok):
    i = pl.program_id(0)
    f = pl.program_id(1)
    tm = x_ref.shape[0]
    active = ta_ref[i] == 1

    def gather_copy(r, tok):
        return pltpu.make_async_copy(h_hbm.at[pl.ds(tok, 1)], x_ref.at[pl.ds(r, 1)], sem_g)

    def scatter_copy(r, dst):
        return pltpu.make_async_copy(acc_ref.at[pl.ds(r, 1)], y_hbm.at[pl.ds(dst, 1)], sem_s)

    @pl.when(active & (f == 0))
    def _():
        cp = pltpu.make_async_copy(dest_hbm.at[i], idx_ref, sem_i)
        cp.start()
        cp.wait()

        def issue(r, c):
            dst = idx_ref[r]
            gather_copy(r, jnp.where(dst < 0, 0, dst % n_tok)).start()
            return c

        lax.fori_loop(0, tm, issue, 0)

        def drain(r, c):
            gather_copy(r, 0).wait()
            return c

        lax.fori_loop(0, tm, drain, 0)
        xb_ref[...] = x_ref[...].astype(BF16)
        acc_ref[...] = jnp.zeros_like(acc_ref)

    @pl.when(active)
    def _():
        acc_ref[...] += _swiglu_step(xb_ref[...], wg_ref[0], wu_ref[0], wd_ref[0])

    @pl.when(active & (f == pl.num_programs(1) - 1))
    def _():
        acc_ref[...] = acc_ref[...] * gate_ref[...]

        def issue(r, c):
            dst = idx_ref[r]

            @pl.when(dst >= 0)
            def _():
                scatter_copy(r, dst).start()

            return c

        lax.fori_loop(0, tm, issue, 0)

        def drain(r, c):
            @pl.when(idx_ref[r] >= 0)
            def _():
                scatter_copy(r, 0).wait()

            return c

        lax.fori_loop(0, tm, drain, 0)


def _moe_experts(h, dest, gate_sorted, tile_expert, tile_active, w_gu, w_down):
    n_tok, d = h.shape
    n_tiles, tm = dest.shape
    ff = w_down.shape[1]
    tf = _tile(ff, TF, LANES)
    nf = ff // tf

    def fe(f, i, ta):
        return jnp.where(ta[i] == 1, f, nf - 1)

    grid_spec = pltpu.PrefetchScalarGridSpec(
        num_scalar_prefetch=2,
        grid=(n_tiles, nf),
        in_specs=[
            pl.BlockSpec(memory_space=pl.ANY),
            pl.BlockSpec(memory_space=pl.ANY),
            pl.BlockSpec((tm, 1), lambda i, f, te, ta: (i, 0)),
            pl.BlockSpec((1, d, tf), lambda i, f, te, ta: (te[i], 0, fe(f, i, ta))),
            pl.BlockSpec((1, d, tf), lambda i, f, te, ta: (te[i], 0, fe(f, i, ta) + nf)),
            pl.BlockSpec((1, tf, d), lambda i, f, te, ta: (te[i], fe(f, i, ta), 0)),
        ],
        out_specs=pl.BlockSpec(memory_space=pl.ANY),
        scratch_shapes=[
            pltpu.SMEM((tm,), jnp.int32),
            pltpu.VMEM((tm, d), F32),
            pltpu.VMEM((tm, d), BF16),
            pltpu.VMEM((tm, d), F32),
            pltpu.SemaphoreType.DMA,
            pltpu.SemaphoreType.DMA,
            pltpu.SemaphoreType.DMA,
        ],
    )
    return pl.pallas_call(
        functools.partial(_moe_kernel, n_tok=n_tok),
        grid_spec=grid_spec,
        out_shape=jax.ShapeDtypeStruct((TOP_K * n_tok, d), F32),
        compiler_params=_params("arbitrary", "arbitrary"),
        name="moe_grouped_swiglu",
    )(tile_expert, tile_active, dest, h, gate_sorted, w_gu, w_gu, w_down)


def _combine_kernel(x_ref, a_ref, b_ref, o_ref):
    o_ref[...] = x_ref[...] + (a_ref[...] + b_ref[...])


def _combine(x, y):
    m, d = x.shape
    tm = _tile(m, TM)
    nb = m // tm
    return pl.pallas_call(
        _combine_kernel,
        grid=(nb,),
        in_specs=[
            pl.BlockSpec((tm, d), lambda i: (i, 0)),
            pl.BlockSpec((tm, d), lambda i: (i, 0)),
            pl.BlockSpec((tm, d), lambda i: (i + nb, 0)),
        ],
        out_specs=pl.BlockSpec((tm, d), lambda i: (i, 0)),
        out_shape=jax.ShapeDtypeStruct((m, d), F32),
        compiler_params=_params("parallel"),
        name="moe_combine",
    )(x, y, y)


def _route_metadata(route, n_tok, tm):
    gates = route[:, :TOP_K].T.reshape(-1)
    experts = route[:, TOP_K:2 * TOP_K].T.reshape(-1).astype(jnp.int32)
    n_pairs = TOP_K * n_tok
    onehot = (experts[:, None] == jnp.arange(N_EXPERTS)[None, :]).astype(jnp.int32)
    rank = jnp.take_along_axis(jnp.cumsum(onehot, axis=0), experts[:, None], axis=1)[:, 0] - 1
    counts = jnp.sum(onehot, axis=0)
    padded = ((counts + tm - 1) // tm) * tm
    ends = jnp.cumsum(padded)
    starts = ends - padded
    rows = starts[experts] + rank
    n_rows = n_pairs + N_EXPERTS * tm
    n_rows = (n_rows // tm) * tm
    dest = jnp.full((n_rows,), -1, jnp.int32).at[rows].set(jnp.arange(n_pairs, dtype=jnp.int32))
    gate_sorted = jnp.zeros((n_rows,), F32).at[rows].set(gates)
    tile_start = jnp.arange(n_rows // tm, dtype=jnp.int32) * tm
    tile_expert = jnp.minimum(jnp.searchsorted(ends, tile_start, side="right"), N_EXPERTS - 1).astype(jnp.int32)
    tile_active = (tile_start < ends[-1]).astype(jnp.int32)
    return dest.reshape(-1, tm), gate_sorted.reshape(-1, 1), tile_expert, tile_active


def _moe(x, gain, w_router, w_gu, w_down):
    n_tok = x.shape[0]
    h, route = _router(x, gain, w_router)
    tm = min(MOE_TM, n_tok)
    dest, gate_sorted, tile_expert, tile_active = _route_metadata(route, n_tok, tm)
    y = _moe_experts(h, dest, gate_sorted, tile_expert, tile_active, w_gu.astype(BF16), w_down.astype(BF16))
    return _combine(x, y)


def _final_norm_kernel(x_ref, g_ref, o_ref):
    o_ref[...] = _rms(x_ref[...], g_ref[...])


def _final_norm(x, gain):
    m, d = x.shape
    tm = _tile(m, TM)
    return pl.pallas_call(
        _final_norm_kernel,
        grid=(m // tm,),
        in_specs=[pl.BlockSpec((tm, d), lambda i: (i, 0)), pl.BlockSpec((1, d), lambda i: (0, 0))],
        out_specs=pl.BlockSpec((tm, d), lambda i: (i, 0)),
        out_shape=jax.ShapeDtypeStruct((m, d), F32),
        compiler_params=_params("parallel"),
        name="final_rmsnorm",
    )(x, gain.reshape(1, d).astype(F32))


def kernel(x, ln_mix, ln_ffn, ln_final, mla_w_in, mla_q_norm, mla_kv_norm, mla_w_uq, mla_w_ukv, mla_w_o, hg_w_in, hg_lower_bound, hg_g_norm, hg_w_o, s5_w_in, s5_a_re, s5_a_im, s5_b_re, s5_b_im, s5_c_re, s5_c_im, s5_d, s5_log_dt, s5_w_out, ffn_w_gu, ffn_w_down, moe_w_router, moe_w_gu, moe_w_down):
    batch, seq, d = x.shape
    depth = ln_mix.shape[0]
    lb_w = jax.nn.softmax(hg_lower_bound.astype(F32), axis=0)
    lower_bounds = jnp.cumsum(lb_w, axis=0) - lb_w[0]
    xt = x.reshape(batch * seq, d).astype(F32)
    for i in range(depth):
        m, j = i % N_MIXERS, i // N_MIXERS
        if m == 0:
            xt = _mla(xt, ln_mix[i], mla_w_in[j], mla_q_norm[j], mla_kv_norm[j], mla_w_uq[j], mla_w_ukv[j],
                      mla_w_o[j], batch, seq)
        elif m == 1:
            xt = _hgrn2(xt, ln_mix[i], hg_w_in[j], lower_bounds[i], hg_g_norm[j], hg_w_o[j], batch, seq)
        else:
            xt = _s5(xt, ln_mix[i], s5_w_in[j], s5_a_re[j], s5_a_im[j], s5_b_re[j], s5_b_im[j], s5_c_re[j],
                     s5_c_im[j], s5_d[j], s5_log_dt[j], s5_w_out[j], batch, seq)
        f = i // 2
        if i % 2 == 0:
            xt = _ffn(xt, ln_ffn[i], ffn_w_gu[f].astype(BF16), ffn_w_down[f].astype(BF16))
        else:
            xt = _moe(xt, ln_ffn[i], moe_w_router[f], moe_w_gu[f], moe_w_down[f])
    return _final_norm(xt, ln_final).reshape(batch, seq, d)
```

```python
import functools
import math

import numpy as np
import jax
import jax.numpy as jnp
from jax import lax
from jax.experimental import pallas as pl
from jax.experimental.pallas import tpu as pltpu

F32 = jnp.float32
BF16 = jnp.bfloat16
EPS = 1e-6
N_MIXERS = 3

MLA_HEADS = 16
MLA_Q_RANK = 512
MLA_KV_RANK = 512
MLA_NOPE = 128
MLA_ROPE = 64
MLA_V = 128
ROPE_THETA = 10000.0
HG_HEADS = 16
HG_CHUNK = 64
S5_GROUP = 16
S5_STATE = 64
S5_BLOCK = 32
N_EXPERTS = 8
TOP_K = 2

LANES = 128
VMEM_LIMIT = 56 * 1024 * 1024

TM = 512
TN = 1024
TF = 512
ATT_TILE = 1024
HG_TILE = 512
MOE_TM = 512
DMA_UNROLL = 8


def _tile(n, t, step=8):
    if n % step:
        return n
    best = step
    for c in range(step, min(n, t) + 1, step):
        if n % c == 0:
            best = c
    return best


def _params(*sem):
    return pltpu.CompilerParams(dimension_semantics=sem, vmem_limit_bytes=VMEM_LIMIT)


def _rms(x, g):
    return x * lax.rsqrt(jnp.mean(x * x, axis=-1, keepdims=True) + EPS) * g


def _gelu_tanh(y):
    c = math.sqrt(2.0 / math.pi)
    return 0.5 * y * (1.0 + jnp.tanh(c * (y + 0.044715 * (y * y * y))))


def _sigmoid(x):
    return 1.0 / (1.0 + jnp.exp(-x))


def _pro_matmul_kernel(x_ref, g_ref, w_ref, o_ref, h_ref, *, prologue):
    @pl.when(pl.program_id(1) == 0)
    def _():
        x = x_ref[...].astype(F32)
        if prologue == "rms":
            x = _rms(x, g_ref[...])
        h_ref[...] = x.astype(BF16)

    o_ref[...] = jnp.dot(h_ref[...], w_ref[...], preferred_element_type=F32).astype(o_ref.dtype)


def _pro_matmul(x, gain, w, *, prologue, out_dtype, x_col_block=0, name):
    m = x.shape[0]
    k, n = w.shape
    tm, tn = _tile(m, TM * max(1, 1024 // k)), _tile(n, TN, LANES)
    return pl.pallas_call(
        functools.partial(_pro_matmul_kernel, prologue=prologue),
        grid=(m // tm, n // tn),
        in_specs=[
            pl.BlockSpec((tm, k), lambda i, j: (i, x_col_block)),
            pl.BlockSpec((1, k), lambda i, j: (0, 0)),
            pl.BlockSpec((k, tn), lambda i, j: (0, j)),
        ],
        out_specs=pl.BlockSpec((tm, tn), lambda i, j: (i, j)),
        out_shape=jax.ShapeDtypeStruct((m, n), out_dtype),
        scratch_shapes=[pltpu.VMEM((tm, k), BF16)],
        compiler_params=_params("parallel", "arbitrary"),
        name=name,
    )(x, gain.reshape(1, k).astype(F32), w)


def _matmul_res_kernel(a_ref, w_ref, r_ref, o_ref):
    o_ref[...] = r_ref[...] + jnp.dot(a_ref[...], w_ref[...], preferred_element_type=F32)


def _matmul_res(a, w, res, *, name):
    m, k = a.shape
    n = w.shape[1]
    tm, tn = _tile(m, TM), _tile(n, TN, LANES)
    return pl.pallas_call(
        _matmul_res_kernel,
        grid=(m // tm, n // tn),
        in_specs=[
            pl.BlockSpec((tm, k), lambda i, j: (i, 0)),
            pl.BlockSpec((k, tn), lambda i, j: (0, j)),
            pl.BlockSpec((tm, tn), lambda i, j: (i, j)),
        ],
        out_specs=pl.BlockSpec((tm, tn), lambda i, j: (i, j)),
        out_shape=jax.ShapeDtypeStruct((m, n), F32),
        compiler_params=_params("parallel", "parallel"),
        name=name,
    )(a, w, res)


def _s5_out_kernel(y_ref, wv_ref, wg_ref, r_ref, o_ref, h_ref):
    @pl.when(pl.program_id(1) == 0)
    def _():
        h_ref[...] = _gelu_tanh(y_ref[...].astype(F32)).astype(BF16)

    h = h_ref[...]
    val = jnp.dot(h, wv_ref[...], preferred_element_type=F32)
    gate = jnp.dot(h, wg_ref[...], preferred_element_type=F32)
    o_ref[...] = r_ref[...] + val * _sigmoid(gate)


def _s5_out(y, w_out, res):
    m, k = y.shape
    n = w_out.shape[1] // 2
    tm, tn = _tile(m, TM), _tile(n, TN, LANES)
    nb = n // tn
    return pl.pallas_call(
        _s5_out_kernel,
        grid=(m // tm, nb),
        in_specs=[
            pl.BlockSpec((tm, k), lambda i, j: (i, 0)),
            pl.BlockSpec((k, tn), lambda i, j: (0, j)),
            pl.BlockSpec((k, tn), lambda i, j: (0, j + nb)),
            pl.BlockSpec((tm, tn), lambda i, j: (i, j)),
        ],
        out_specs=pl.BlockSpec((tm, tn), lambda i, j: (i, j)),
        out_shape=jax.ShapeDtypeStruct((m, n), F32),
        scratch_shapes=[pltpu.VMEM((tm, k), BF16)],
        compiler_params=_params("parallel", "arbitrary"),
        name="s5_out_glu",
    )(y, w_out, w_out, res)


def _swiglu_step(h, wg, wu, wd):
    g = jnp.dot(h, wg, preferred_element_type=F32)
    u = jnp.dot(h, wu, preferred_element_type=F32)
    a = (g * _sigmoid(g) * u).astype(BF16)
    return jnp.dot(a, wd, preferred_element_type=F32)


def _ffn_kernel(x_ref, g_ref, wg_ref, wu_ref, wd_ref, o_ref, h_ref, acc_ref):
    f = pl.program_id(1)

    @pl.when(f == 0)
    def _():
        h_ref[...] = _rms(x_ref[...], g_ref[...]).astype(BF16)
        acc_ref[...] = jnp.zeros_like(acc_ref)

    acc_ref[...] += _swiglu_step(h_ref[...], wg_ref[...], wu_ref[...], wd_ref[...])

    @pl.when(f == pl.num_programs(1) - 1)
    def _():
        o_ref[...] = x_ref[...] + acc_ref[...]


def _ffn(x, gain, w_gu, w_down):
    m, d = x.shape
    ff = w_down.shape[0]
    tm, tf = _tile(m, TM), _tile(ff, TF, LANES)
    nf = ff // tf
    return pl.pallas_call(
        _ffn_kernel,
        grid=(m // tm, nf),
        in_specs=[
            pl.BlockSpec((tm, d), lambda i, f: (i, 0)),
            pl.BlockSpec((1, d), lambda i, f: (0, 0)),
            pl.BlockSpec((d, tf), lambda i, f: (0, f)),
            pl.BlockSpec((d, tf), lambda i, f: (0, f + nf)),
            pl.BlockSpec((tf, d), lambda i, f: (f, 0)),
        ],
        out_specs=pl.BlockSpec((tm, d), lambda i, f: (i, 0)),
        out_shape=jax.ShapeDtypeStruct((m, d), F32),
        scratch_shapes=[pltpu.VMEM((tm, d), BF16), pltpu.VMEM((tm, d), F32)],
        compiler_params=_params("parallel", "arbitrary"),
        name="ffn_swiglu",
    )(x, gain.reshape(1, d).astype(F32), w_gu, w_gu, w_down)


def _rope_table(seq):
    half = MLA_ROPE // 2
    inv = ROPE_THETA ** (-jnp.arange(half, dtype=F32) / half)
    ang = jnp.arange(seq, dtype=F32)[:, None] * inv
    cos, sin = jnp.cos(ang), jnp.sin(ang)
    return jnp.concatenate([cos, cos, sin, sin], axis=-1)


def _rot_cols(w):
    half = w.shape[-1] // 2
    return jnp.concatenate([-w[..., half:], w[..., :half]], axis=-1)


def _rope_slab(slab, cs):
    prod = slab * cs
    return prod + pltpu.roll(prod, MLA_ROPE, axis=1)


def _krope_kernel(p_ref, cs_ref, o_ref):
    r = _rope_slab(p_ref[...].astype(F32), cs_ref[...])
    lane = lax.broadcasted_iota(jnp.int32, r.shape, 1)
    o_ref[...] = jnp.where(lane < MLA_ROPE, r, 0.0).astype(o_ref.dtype)


def _krope(proj, cs, seq, col_block):
    t = proj.shape[0]
    tm = _tile(seq, TM)
    ns = seq // tm
    return pl.pallas_call(
        _krope_kernel,
        grid=(t // tm,),
        in_specs=[
            pl.BlockSpec((tm, LANES), lambda i: (i, col_block)),
            pl.BlockSpec((tm, LANES), lambda i: (i % ns, 0)),
        ],
        out_specs=pl.BlockSpec((tm, LANES), lambda i: (i, 0)),
        out_shape=jax.ShapeDtypeStruct((t, LANES), BF16),
        compiler_params=_params("parallel"),
        name="mla_k_rope",
    )(proj, cs)


def _attn_kernel(qi_ref, ki_ref, q_ref, cs_ref, kn_ref, kr_ref, v_ref, o_ref,
                 qc_ref, m_ref, l_ref, acc_ref):
    s_id = pl.program_id(2)
    qi = qi_ref[s_id]
    ki = ki_ref[s_id]

    @pl.when(ki == 0)
    def _():
        q = q_ref[...]
        qr = _rope_slab(q[:, LANES:].astype(F32), cs_ref[...]).astype(BF16)
        qc_ref[...] = jnp.concatenate([q[:, :LANES], qr], axis=1)
        m_ref[...] = jnp.full_like(m_ref, -jnp.inf)
        l_ref[...] = jnp.zeros_like(l_ref)
        acc_ref[...] = jnp.zeros_like(acc_ref)

    kc = jnp.concatenate([kn_ref[...], kr_ref[...]], axis=1)
    s = lax.dot_general(qc_ref[...], kc, (((1,), (1,)), ((), ())), preferred_element_type=F32)

    def update(sc):
        m_prev = m_ref[...]
        m_new = jnp.maximum(m_prev, jnp.max(sc, axis=-1, keepdims=True))
        alpha = jnp.exp2(m_prev - m_new)
        p = jnp.exp2(sc - m_new)
        l_ref[...] = alpha * l_ref[...] + jnp.sum(p, axis=-1, keepdims=True)
        acc_ref[...] = alpha * acc_ref[...] + jnp.dot(p.astype(BF16), v_ref[...], preferred_element_type=F32)
        m_ref[...] = m_new

    @pl.when(ki < qi)
    def _():
        update(s)

    @pl.when(ki == qi)
    def _():
        row = lax.broadcasted_iota(jnp.int32, s.shape, 0)
        col = lax.broadcasted_iota(jnp.int32, s.shape, 1)
        update(jnp.where(col <= row, s, -jnp.inf))
        o_ref[...] = (acc_ref[...] / l_ref[...]).astype(o_ref.dtype)


def _attention(q_ext, kv_ext, k_rope, cs, batch, seq):
    t = q_ext.shape[0]
    heads = q_ext.shape[1] // (2 * LANES)
    tq = _tile(seq, ATT_TILE)
    nq = seq // tq
    steps = [(i, j) for i in range(nq) for j in range(i + 1)]
    qi_tab = jnp.asarray([s[0] for s in steps], jnp.int32)
    ki_tab = jnp.asarray([s[1] for s in steps], jnp.int32)
    grid_spec = pltpu.PrefetchScalarGridSpec(
        num_scalar_prefetch=2,
        grid=(batch, heads, len(steps)),
        in_specs=[
            pl.BlockSpec((tq, 2 * LANES), lambda b, h, s, qi, ki: (b * nq + qi[s], h)),
            pl.BlockSpec((tq, LANES), lambda b, h, s, qi, ki: (qi[s], 0)),
            pl.BlockSpec((tq, LANES), lambda b, h, s, qi, ki: (b * nq + ki[s], 2 * h)),
            pl.BlockSpec((tq, LANES), lambda b, h, s, qi, ki: (b * nq + ki[s], 0)),
            pl.BlockSpec((tq, LANES), lambda b, h, s, qi, ki: (b * nq + ki[s], 2 * h + 1)),
        ],
        out_specs=pl.BlockSpec((tq, LANES), lambda b, h, s, qi, ki: (b * nq + qi[s], h)),
        scratch_shapes=[
            pltpu.VMEM((tq, 2 * LANES), BF16),
            pltpu.VMEM((tq, 1), F32),
            pltpu.VMEM((tq, 1), F32),
            pltpu.VMEM((tq, LANES), F32),
        ],
    )
    return pl.pallas_call(
        _attn_kernel,
        grid_spec=grid_spec,
        out_shape=jax.ShapeDtypeStruct((t, heads * LANES), BF16),
        compiler_params=_params("parallel", "parallel", "arbitrary"),
        name="mla_flash_attention",
    )(qi_tab, ki_tab, q_ext, cs, kv_ext, k_rope, kv_ext)


def _mla(x, ln, w_in, q_norm, kv_norm, w_uq, w_ukv, w_o, batch, seq):
    heads = MLA_HEADS
    assert MLA_NOPE == LANES and MLA_V == LANES and 2 * MLA_ROPE == LANES
    assert MLA_Q_RANK == MLA_KV_RANK
    rank = MLA_Q_RANK
    w_kr = w_in[:, 2 * rank:]
    w_in_ext = jnp.concatenate([w_in, _rot_cols(w_kr)], axis=1).astype(BF16)
    proj = _pro_matmul(x, ln, w_in_ext, prologue="rms", out_dtype=F32, name="mla_in_proj")
    scale = (MLA_NOPE + MLA_ROPE) ** -0.5 * math.log2(math.e)
    wq = w_uq.reshape(rank, heads, MLA_NOPE + MLA_ROPE) * scale
    wq_ext = jnp.concatenate([wq, _rot_cols(wq[..., MLA_NOPE:])], axis=-1)
    wq_ext = wq_ext.reshape(rank, heads * 2 * LANES).astype(BF16)
    q_ext = _pro_matmul(proj, q_norm, wq_ext, prologue="rms", out_dtype=BF16, x_col_block=0, name="mla_q_up")
    kv_ext = _pro_matmul(proj, kv_norm, w_ukv.astype(BF16), prologue="rms", out_dtype=BF16, x_col_block=1,
                         name="mla_kv_up")
    cs = _rope_table(seq)
    k_rope = _krope(proj, cs, seq, col_block=2 * rank // LANES)
    o = _attention(q_ext, kv_ext, k_rope, cs, batch, seq)
    return _matmul_res(o, w_o.astype(BF16), x, name="mla_out_proj")


def _hg_tables(c):
    levels = int(math.log2(c))
    assert 2 ** levels == c
    t = np.arange(c)[:, None]
    u = np.arange(c)[None, :]
    mats = []
    for l in range(1, levels + 1):
        mid = ((t >> l) << l) + (1 << (l - 1))
        upper = t >= mid
        mats.append(np.where(upper, (u >= mid) & (u <= t), (u > t) & (u <= mid - 1)))
    mats.append(u <= t)
    mats.append(u > t)
    masks = [t == u]
    for l in range(1, levels + 1):
        masks.append(((t >> l) == (u >> l)) & (((t >> (l - 1)) & 1) == 1) & (((u >> (l - 1)) & 1) == 0))
    return (np.concatenate(mats, axis=0).astype(np.float32), np.concatenate(masks, axis=0).astype(np.float32),
            levels)


def _split3(x):
    hi = x.astype(BF16)
    r = x - hi.astype(F32)
    mid = r.astype(BF16)
    lo = (r - mid.astype(F32)).astype(BF16)
    return hi, mid, lo


def _hgrn2_kernel(q_ref, z_ref, v_ref, g_ref, lb_ref, gn_ref, tab_ref, msk_ref, o_ref, st_ref, *, chunk, levels):
    c = chunk
    n_chunks = q_ref.shape[0] // c
    dk = q_ref.shape[1]

    @pl.when(pl.program_id(2) == 0)
    def _():
        st_ref[...] = jnp.zeros_like(st_ref)

    lb = lb_ref[...]
    log_lb = jnp.log(lb)
    log_1mlb = jnp.log1p(-lb)
    z = z_ref[...]
    q = q_ref[...].astype(F32)
    e = jnp.exp(-jnp.abs(z))
    log_sig = jnp.minimum(z, 0.0) - jnp.log1p(e)
    bterm = log_1mlb + log_sig
    lf = jnp.maximum(log_lb, bterm) + jnp.log1p(jnp.exp(-jnp.abs(log_lb - bterm)))
    k = (1.0 - lb) * jnp.where(z >= 0, e, 1.0) / (1.0 + e)

    def chunks_on_lanes(x):
        return jnp.concatenate([x[i * c:(i + 1) * c] for i in range(n_chunks)], axis=1)

    tab = tab_ref[...]
    p0, p1, p2 = _split3(lf)
    sums = (jnp.dot(tab, chunks_on_lanes(p0), preferred_element_type=F32)
            + jnp.dot(tab, chunks_on_lanes(p1), preferred_element_type=F32)
            + jnp.dot(tab, chunks_on_lanes(p2), preferred_element_type=F32))
    dec = jnp.exp(sums)
    diag = jnp.sum(q * k, axis=-1, keepdims=True)
    o_intra, q_in, upds, d_last = [], [], [], []
    for i in range(n_chunks):
        rows = slice(i * c, (i + 1) * c)
        qi, ki, vi = q[rows], k[rows], v_ref[rows, :]
        di = dec[:, i * dk:(i + 1) * dk]
        scores = msk_ref[0:c, :] * diag[rows]
        for l in range(1, levels + 1):
            d = di[(l - 1) * c:l * c]
            sl_ = lax.dot_general((qi * d).astype(BF16), (ki * d).astype(BF16), (((1,), (1,)), ((), ())),
                                  preferred_element_type=F32)
            scores = scores + msk_ref[l * c:(l + 1) * c, :] * sl_
        d_pre = di[levels * c:(levels + 1) * c]
        d_suf = di[(levels + 1) * c:(levels + 2) * c]
        o_intra.append(jnp.dot(scores.astype(BF16), vi, preferred_element_type=F32))
        q_in.append((qi * d_pre).astype(BF16))
        upds.append(lax.dot_general(vi, (ki * d_suf).astype(BF16), (((0,), (0,)), ((), ())),
                                    preferred_element_type=F32))
        d_last.append(d_pre[c - 1:c, :])
    st = st_ref[...]
    outs = []
    for i in range(n_chunks):
        outs.append(o_intra[i] + lax.dot_general(q_in[i], st.astype(BF16), (((1,), (1,)), ((), ())),
                                                 preferred_element_type=F32))
        st = st * d_last[i] + upds[i]
    st_ref[...] = st
    o = jnp.concatenate(outs, axis=0)
    gt = g_ref[...].astype(F32)
    o_ref[...] = (_rms(o, gn_ref[...]) * (gt * _sigmoid(gt))).astype(o_ref.dtype)


def _hgrn2_core(qvg, z, lb, g_norm, batch, seq):
    t = z.shape[0]
    heads = z.shape[1] // LANES
    ts = _tile(seq, HG_TILE)
    ns = seq // ts
    c = min(HG_CHUNK, ts)
    tab, msk, levels = _hg_tables(c)
    tab = jnp.asarray(tab, BF16)
    msk = jnp.asarray(msk, F32)
    return pl.pallas_call(
        functools.partial(_hgrn2_kernel, chunk=c, levels=levels),
        grid=(batch, heads, ns),
        in_specs=[
            pl.BlockSpec((ts, LANES), lambda b, h, s: (b * ns + s, h)),
            pl.BlockSpec((ts, LANES), lambda b, h, s: (b * ns + s, h)),
            pl.BlockSpec((ts, LANES), lambda b, h, s: (b * ns + s, heads + h)),
            pl.BlockSpec((ts, LANES), lambda b, h, s: (b * ns + s, 2 * heads + h)),
            pl.BlockSpec((1, LANES), lambda b, h, s: (0, h)),
            pl.BlockSpec((1, LANES), lambda b, h, s: (0, 0)),
            pl.BlockSpec(tab.shape, lambda b, h, s: (0, 0)),
            pl.BlockSpec(msk.shape, lambda b, h, s: (0, 0)),
        ],
        out_specs=pl.BlockSpec((ts, LANES), lambda b, h, s: (b * ns + s, h)),
        out_shape=jax.ShapeDtypeStruct((t, heads * LANES), BF16),
        scratch_shapes=[pltpu.VMEM((LANES, LANES), F32)],
        compiler_params=_params("parallel", "parallel", "arbitrary"),
        name="hgrn2_recurrence",
    )(qvg, z, qvg, qvg, lb.reshape(1, -1).astype(F32), g_norm.reshape(1, -1).astype(F32), tab, msk)


def _hgrn2(x, ln, w_in, lb, g_norm, w_o, batch, seq):
    d = x.shape[1]
    assert d == HG_HEADS * LANES
    w_qvg = jnp.concatenate([w_in[:, :d], w_in[:, 2 * d:]], axis=1).astype(BF16)
    w_z = w_in[:, d:2 * d].astype(BF16)
    qvg = _pro_matmul(x, ln, w_qvg, prologue="rms", out_dtype=BF16, name="hg_in_proj_qvg")
    z = _pro_matmul(x, ln, w_z, prologue="rms", out_dtype=F32, name="hg_in_proj_z")
    o = _hgrn2_core(qvg, z, lb, g_norm, batch, seq)
    return _matmul_res(o, w_o.astype(BF16), x, name="hg_out_proj")


def _s5_tables(a_re, a_im, b_re, b_im, c_re, c_im, d_skip, log_dt, nblk):
    L = S5_BLOCK
    G, P = a_re.shape
    N = b_re.shape[-1]
    a = lax.complex(a_re.astype(F32), a_im.astype(F32))
    dt_a = a * jnp.exp(log_dt.astype(F32))[:, None]
    a_bar = jnp.exp(dt_a)
    b_bar = ((a_bar - 1.0) / a)[:, :, None] * lax.complex(b_re.astype(F32), b_im.astype(F32))
    cc = lax.complex(c_re.astype(F32), c_im.astype(F32))
    pw = jnp.exp(jnp.arange(L + 1, dtype=F32)[:, None, None] * dt_a)
    kern = jnp.real(jnp.einsum("gnp,tgp,gpm->gtnm", cc, pw[:L], b_bar))
    kern = kern.at[:, 0].add(jax.vmap(jnp.diag)(d_skip.astype(F32)))
    s_idx = jnp.arange(L)[:, None]
    t_idx = jnp.arange(L)[None, :]
    lag = t_idx - s_idx
    toep = jnp.where((lag >= 0)[None, :, :, None, None], kern[:, jnp.clip(lag, 0, L - 1)], 0.0)
    toep = toep.transpose(0, 1, 4, 2, 3).reshape(G, L * N, L * N)
    wst = jnp.einsum("sgp,gpm->gsmp", pw[:L][::-1], b_bar)
    w_st = jnp.concatenate([jnp.real(wst), jnp.imag(wst)], axis=-1).reshape(G, L * N, 2 * P)
    co = jnp.einsum("gnp,tgp->gptn", cc, pw[1:])
    c_out = jnp.concatenate([jnp.real(co), -jnp.imag(co)], axis=1).reshape(G, 2 * P, L * N)
    nsteps = max(1, int(math.ceil(math.log2(nblk))))
    lam_k = jnp.exp((L * 2.0 ** jnp.arange(nsteps, dtype=F32))[None, :, None] * dt_a[:, None, :])
    lr, li = jnp.real(lam_k), jnp.imag(lam_k)
    lam = jnp.stack([jnp.concatenate([lr, lr], -1), jnp.concatenate([-li, li], -1)], axis=2)
    return toep.astype(BF16), w_st.astype(BF16), c_out.astype(BF16), lam.astype(F32)


def _s5_kernel(u_ref, toep_ref, wst_ref, cout_ref, lam_ref, y_ref, *, nblk, nsteps):
    u = u_ref[0]
    z = jnp.dot(u, wst_ref[0], preferred_element_type=F32)
    rows, p2 = z.shape
    blk = lax.broadcasted_iota(jnp.int32, (rows, p2), 0) % nblk
    h = z
    for k in range(nsteps):
        sh = 1 << k
        prev = jnp.where(blk >= sh, pltpu.roll(h, sh, axis=0), 0.0)
        lam = lam_ref[0, k]
        h = h + prev * lam[0:1, :] + pltpu.roll(prev, p2 // 2, axis=1) * lam[1:2, :]
    h0 = jnp.where(blk >= 1, pltpu.roll(h, 1, axis=0), 0.0)
    y = jnp.dot(u, toep_ref[0], preferred_element_type=F32)
    y = y + jnp.dot(h0.astype(BF16), cout_ref[0], preferred_element_type=F32)
    y_ref[0] = y.astype(y_ref.dtype)


def _s5_core(u_g, toep, w_st, c_out, lam, nblk):
    g, rows, ln = u_g.shape
    p2 = w_st.shape[-1]
    nsteps = lam.shape[1]
    return pl.pallas_call(
        functools.partial(_s5_kernel, nblk=nblk, nsteps=nsteps),
        grid=(g,),
        in_specs=[
            pl.BlockSpec((1, rows, ln), lambda i: (i, 0, 0)),
            pl.BlockSpec((1, ln, ln), lambda i: (i, 0, 0)),
            pl.BlockSpec((1, ln, p2), lambda i: (i, 0, 0)),
            pl.BlockSpec((1, p2, ln), lambda i: (i, 0, 0)),
            pl.BlockSpec((1, nsteps, 2, p2), lambda i: (i, 0, 0, 0)),
        ],
        out_specs=pl.BlockSpec((1, rows, ln), lambda i: (i, 0, 0)),
        out_shape=jax.ShapeDtypeStruct((g, rows, ln), BF16),
        compiler_params=_params("parallel"),
        name="s5_ssm",
    )(u_g, toep, w_st, c_out, lam)


def _s5(x, ln, w_in, a_re, a_im, b_re, b_im, c_re, c_im, d_skip, log_dt, w_out, batch, seq):
    t, d = x.shape
    G, N, L = d // S5_GROUP, S5_GROUP, S5_BLOCK
    nblk = seq // L
    u = _pro_matmul(x, ln, w_in.astype(BF16), prologue="rms", out_dtype=BF16, name="s5_in_proj")
    toep, w_st, c_out, lam = _s5_tables(a_re, a_im, b_re, b_im, c_re, c_im, d_skip, log_dt, nblk)
    u_g = u.reshape(batch, nblk, L, G, N).transpose(3, 0, 1, 2, 4).reshape(G, batch * nblk, L * N)
    y_g = _s5_core(u_g, toep, w_st, c_out, lam, nblk)
    y = y_g.reshape(G, batch, nblk, L, N).transpose(1, 2, 3, 0, 4).reshape(t, d)
    return _s5_out(y, w_out.astype(BF16), x)


def _router_kernel(x_ref, g_ref, w_ref, h_ref, r_ref):
    h = _rms(x_ref[...], g_ref[...])
    h_ref[...] = h
    h0 = h.astype(BF16)
    h1 = (h - h0.astype(F32)).astype(BF16)
    w = w_ref[...]
    w0 = w.astype(BF16)
    w1 = (w - w0.astype(F32)).astype(BF16)
    logits = (jnp.dot(h0, w0, preferred_element_type=F32) + jnp.dot(h0, w1, preferred_element_type=F32)
              + jnp.dot(h1, w0, preferred_element_type=F32))
    lane = lax.broadcasted_iota(jnp.int32, logits.shape, 1).astype(F32)
    neg = -jnp.inf
    lg = jnp.where(lane < N_EXPERTS, logits, neg)
    m1 = jnp.max(lg, axis=-1, keepdims=True)
    i1 = jnp.min(jnp.where(lg == m1, lane, float(LANES)), axis=-1, keepdims=True)
    lg2 = jnp.where(lane == i1, neg, lg)
    m2 = jnp.max(lg2, axis=-1, keepdims=True)
    i2 = jnp.min(jnp.where(lg2 == m2, lane, float(LANES)), axis=-1, keepdims=True)
    e2 = jnp.exp(m2 - m1)
    g1 = 1.0 / (1.0 + e2)
    g2 = e2 / (1.0 + e2)
    out = jnp.where(lane == 0, g1, jnp.where(lane == 1, g2, jnp.where(lane == 2, i1, jnp.where(lane == 3, i2, 0.0))))
    r_ref[...] = out[:, :r_ref.shape[1]]


def _router(x, gain, w_router):
    m, d = x.shape
    tm = _tile(m, TM)
    w_pad = jnp.zeros((d, LANES), F32).at[:, :N_EXPERTS].set(w_router.astype(F32))
    return pl.pallas_call(
        _router_kernel,
        grid=(m // tm,),
        in_specs=[
            pl.BlockSpec((tm, d), lambda i: (i, 0)),
            pl.BlockSpec((1, d), lambda i: (0, 0)),
            pl.BlockSpec((d, LANES), lambda i: (0, 0)),
        ],
        out_specs=[pl.BlockSpec((tm, d), lambda i: (i, 0)), pl.BlockSpec((tm, 8), lambda i: (i, 0))],
        out_shape=[jax.ShapeDtypeStruct((m, d), F32), jax.ShapeDtypeStruct((m, 8), F32)],
        compiler_params=_params("parallel"),
        name="moe_router_top2",
    )(x, gain.reshape(1, d).astype(F32), w_pad)


def _moe_kernel(te_ref, ta_ref, idx_hbm, h_hbm, wg_ref, wu_ref, wd_ref, y_hbm,
                idx_ref, x_ref, xb_ref, acc_ref, sem_i, sem_g, sem_s):
    i = pl.program_id(0)
    f = pl.program_id(1)
    tm = x_ref.shape[0]
    active = ta_ref[i] == 1

    def gather_copy(r, tok):
        return pltpu.make_async_copy(h_hbm.at[pl.ds(tok, 1)], x_ref.at[pl.ds(r, 1)], sem_g)

    def scatter_copy(r, dst):
        return pltpu.make_async_copy(acc_ref.at[pl.ds(r, 1)], y_hbm.at[pl.ds(dst, 1)], sem_s)

    @pl.when(active & (f == 0))
    def _():
        cp = pltpu.make_async_copy(idx_hbm.at[i], idx_ref, sem_i)
        cp.start()
        cp.wait()

        def issue(r, c):
            gather_copy(r, idx_ref[0, r]).start()
            return c

        lax.fori_loop(0, tm, issue, 0, unroll=DMA_UNROLL)

        def drain(r, c):
            gather_copy(r, 0).wait()
            return c

        lax.fori_loop(0, tm, drain, 0, unroll=DMA_UNROLL)
        xb_ref[...] = x_ref[...].astype(BF16)
        acc_ref[...] = jnp.zeros_like(acc_ref)

    @pl.when(active)
    def _():
        acc_ref[...] += _swiglu_step(xb_ref[...], wg_ref[0], wu_ref[0], wd_ref[0])

    @pl.when(active & (f == pl.num_programs(1) - 1))
    def _():
        def issue(r, c):
            dst = idx_ref[1, r]

            @pl.when(dst >= 0)
            def _():
                scatter_copy(r, dst).start()

            return c

        lax.fori_loop(0, tm, issue, 0, unroll=DMA_UNROLL)

        def drain(r, c):
            @pl.when(idx_ref[1, r] >= 0)
            def _():
                scatter_copy(r, 0).wait()

            return c

        lax.fori_loop(0, tm, drain, 0, unroll=DMA_UNROLL)


def _moe_experts(h, idx, tile_expert, tile_active, w_gu, w_down):
    n_tok, d = h.shape
    n_tiles, _, tm = idx.shape
    ff = w_down.shape[1]
    tf = _tile(ff, TF, LANES)
    nf = ff // tf

    def fe(f, i, ta):
        return jnp.where(ta[i] == 1, f, nf - 1)

    grid_spec = pltpu.PrefetchScalarGridSpec(
        num_scalar_prefetch=2,
        grid=(n_tiles, nf),
        in_specs=[
            pl.BlockSpec(memory_space=pl.ANY),
            pl.BlockSpec(memory_space=pl.ANY),
            pl.BlockSpec((1, d, tf), lambda i, f, te, ta: (te[i], 0, fe(f, i, ta))),
            pl.BlockSpec((1, d, tf), lambda i, f, te, ta: (te[i], 0, fe(f, i, ta) + nf)),
            pl.BlockSpec((1, tf, d), lambda i, f, te, ta: (te[i], fe(f, i, ta), 0)),
        ],
        out_specs=pl.BlockSpec(memory_space=pl.ANY),
        scratch_shapes=[
            pltpu.SMEM((2, tm), jnp.int32),
            pltpu.VMEM((tm, d), F32),
            pltpu.VMEM((tm, d), BF16),
            pltpu.VMEM((tm, d), F32),
            pltpu.SemaphoreType.DMA,
            pltpu.SemaphoreType.DMA,
            pltpu.SemaphoreType.DMA,
        ],
    )
    return pl.pallas_call(
        _moe_kernel,
        grid_spec=grid_spec,
        out_shape=jax.ShapeDtypeStruct((TOP_K * n_tok, d), F32),
        compiler_params=_params("arbitrary", "arbitrary"),
        name="moe_grouped_swiglu",
    )(tile_expert, tile_active, idx, h, w_gu, w_gu, w_down)


def _combine_kernel(x_ref, a_ref, b_ref, r_ref, o_ref):
    r = r_ref[...]
    o_ref[...] = x_ref[...] + (r[:, 0:1] * a_ref[...] + r[:, 1:2] * b_ref[...])


def _combine(x, y, route):
    m, d = x.shape
    tm = _tile(m, TM)
    nb = m // tm
    return pl.pallas_call(
        _combine_kernel,
        grid=(nb,),
        in_specs=[
            pl.BlockSpec((tm, d), lambda i: (i, 0)),
            pl.BlockSpec((tm, d), lambda i: (i, 0)),
            pl.BlockSpec((tm, d), lambda i: (i + nb, 0)),
            pl.BlockSpec((tm, route.shape[1]), lambda i: (i, 0)),
        ],
        out_specs=pl.BlockSpec((tm, d), lambda i: (i, 0)),
        out_shape=jax.ShapeDtypeStruct((m, d), F32),
        compiler_params=_params("parallel"),
        name="moe_combine",
    )(x, y, y, route)


def _route_metadata(route, n_tok, tm):
    experts = route[:, TOP_K:2 * TOP_K].T.reshape(-1).astype(jnp.int32)
    n_pairs = TOP_K * n_tok
    onehot = (experts[:, None] == jnp.arange(N_EXPERTS)[None, :]).astype(jnp.int32)
    rank = jnp.take_along_axis(jnp.cumsum(onehot, axis=0), experts[:, None], axis=1)[:, 0] - 1
    counts = jnp.sum(onehot, axis=0)
    padded = ((counts + tm - 1) // tm) * tm
    ends = jnp.cumsum(padded)
    starts = ends - padded
    rows = starts[experts] + rank
    n_rows = n_pairs + N_EXPERTS * tm
    n_rows = (n_rows // tm) * tm
    dest = jnp.full((n_rows,), -1, jnp.int32).at[rows].set(jnp.arange(n_pairs, dtype=jnp.int32))
    src = jnp.where(dest < 0, 0, dest % n_tok)
    idx = jnp.stack([src.reshape(-1, tm), dest.reshape(-1, tm)], axis=1)
    tile_start = jnp.arange(n_rows // tm, dtype=jnp.int32) * tm
    tile_expert = jnp.sum((tile_start[:, None] >= ends[None, :]).astype(jnp.int32), axis=1)
    tile_expert = jnp.minimum(tile_expert, N_EXPERTS - 1)
    tile_active = (tile_start < ends[-1]).astype(jnp.int32)
    return idx, tile_expert, tile_active


def _moe(x, gain, w_router, w_gu, w_down):
    n_tok = x.shape[0]
    h, route = _router(x, gain, w_router)
    tm = min(MOE_TM, n_tok)
    idx, tile_expert, tile_active = _route_metadata(route, n_tok, tm)
    y = _moe_experts(h, idx, tile_expert, tile_active, w_gu.astype(BF16), w_down.astype(BF16))
    return _combine(x, y, route)


def _final_norm_kernel(x_ref, g_ref, o_ref):
    o_ref[...] = _rms(x_ref[...], g_ref[...])


def _final_norm(x, gain):
    m, d = x.shape
    tm = _tile(m, TM)
    return pl.pallas_call(
        _final_norm_kernel,
        grid=(m // tm,),
        in_specs=[pl.BlockSpec((tm, d), lambda i: (i, 0)), pl.BlockSpec((1, d), lambda i: (0, 0))],
        out_specs=pl.BlockSpec((tm, d), lambda i: (i, 0)),
        out_shape=jax.ShapeDtypeStruct((m, d), F32),
        compiler_params=_params("parallel"),
        name="final_rmsnorm",
    )(x, gain.reshape(1, d).astype(F32))


def kernel(x, ln_mix, ln_ffn, ln_final, mla_w_in, mla_q_norm, mla_kv_norm, mla_w_uq, mla_w_ukv, mla_w_o, hg_w_in, hg_lower_bound, hg_g_norm, hg_w_o, s5_w_in, s5_a_re, s5_a_im, s5_b_re, s5_b_im, s5_c_re, s5_c_im, s5_d, s5_log_dt, s5_w_out, ffn_w_gu, ffn_w_down, moe_w_router, moe_w_gu, moe_w_down):
    batch, seq, d = x.shape
    depth = ln_mix.shape[0]
    lb_w = jax.nn.softmax(hg_lower_bound.astype(F32), axis=0)
    lower_bounds = jnp.cumsum(lb_w, axis=0) - lb_w[0]
    xt = x.reshape(batch * seq, d).astype(F32)
    for i in range(depth):
        m, j = i % N_MIXERS, i // N_MIXERS
        if m == 0:
            xt = _mla(xt, ln_mix[i], mla_w_in[j], mla_q_norm[j], mla_kv_norm[j], mla_w_uq[j], mla_w_ukv[j],
                      mla_w_o[j], batch, seq)
        elif m == 1:
            xt = _hgrn2(xt, ln_mix[i], hg_w_in[j], lower_bounds[i], hg_g_norm[j], hg_w_o[j], batch, seq)
        else:
            xt = _s5(xt, ln_mix[i], s5_w_in[j], s5_a_re[j], s5_a_im[j], s5_b_re[j], s5_b_im[j], s5_c_re[j],
                     s5_c_im[j], s5_d[j], s5_log_dt[j], s5_w_out[j], batch, seq)
        f = i // 2
        if i % 2 == 0:
            xt = _ffn(xt, ln_ffn[i], ffn_w_gu[f].astype(BF16), ffn_w_down[f].astype(BF16))
        else:
            xt = _moe(xt, ln_ffn[i], moe_w_router[f], moe_w_gu[f], moe_w_down[f])
    return _final_norm(xt, ln_final).reshape(batch, seq, d)
```

```python
import functools
import math

import numpy as np
import jax
import jax.numpy as jnp
from jax import lax
from jax.experimental import pallas as pl
from jax.experimental.pallas import tpu as pltpu

F32 = jnp.float32
BF16 = jnp.bfloat16
EPS = 1e-6
N_MIXERS = 3

MLA_HEADS = 16
MLA_Q_RANK = 512
MLA_KV_RANK = 512
MLA_NOPE = 128
MLA_ROPE = 64
MLA_V = 128
ROPE_THETA = 10000.0
HG_HEADS = 16
HG_CHUNK = 64
S5_GROUP = 16
S5_STATE = 64
S5_BLOCK = 32
N_EXPERTS = 8
TOP_K = 2

LANES = 128
VMEM_LIMIT = 56 * 1024 * 1024

TM = 512
TN = 1024
TF = 512
ATT_TILE = 1024
ATT_GROUPS = 4
HG_TILE = 512
MOE_TM = 512
DMA_UNROLL = 8


def _tile(n, t, step=8):
    if n % step:
        return n
    best = step
    for c in range(step, min(n, t) + 1, step):
        if n % c == 0:
            best = c
    return best


def _params(*sem):
    return pltpu.CompilerParams(dimension_semantics=sem, vmem_limit_bytes=VMEM_LIMIT)


def _rms(x, g):
    return x * lax.rsqrt(jnp.mean(x * x, axis=-1, keepdims=True) + EPS) * g


def _gelu_tanh(y):
    c = math.sqrt(2.0 / math.pi)
    return 0.5 * y * (1.0 + jnp.tanh(c * (y + 0.044715 * (y * y * y))))


def _sigmoid(x):
    return 1.0 / (1.0 + jnp.exp(-x))


def _pro_matmul_kernel(x_ref, g_ref, w_ref, o_ref, h_ref, *, prologue):
    @pl.when(pl.program_id(1) == 0)
    def _():
        x = x_ref[...].astype(F32)
        if prologue == "rms":
            x = _rms(x, g_ref[...])
        h_ref[...] = x.astype(BF16)

    o_ref[...] = jnp.dot(h_ref[...], w_ref[...], preferred_element_type=F32).astype(o_ref.dtype)


def _pro_matmul(x, gain, w, *, prologue, out_dtype, x_col_block=0, name):
    m = x.shape[0]
    k, n = w.shape
    tm, tn = _tile(m, TM * max(1, 1024 // k)), _tile(n, TN, LANES)
    return pl.pallas_call(
        functools.partial(_pro_matmul_kernel, prologue=prologue),
        grid=(m // tm, n // tn),
        in_specs=[
            pl.BlockSpec((tm, k), lambda i, j: (i, x_col_block)),
            pl.BlockSpec((1, k), lambda i, j: (0, 0)),
            pl.BlockSpec((k, tn), lambda i, j: (0, j)),
        ],
        out_specs=pl.BlockSpec((tm, tn), lambda i, j: (i, j)),
        out_shape=jax.ShapeDtypeStruct((m, n), out_dtype),
        scratch_shapes=[pltpu.VMEM((tm, k), BF16)],
        compiler_params=_params("parallel", "arbitrary"),
        name=name,
    )(x, gain.reshape(1, k).astype(F32), w)


def _matmul_res_kernel(a_ref, w_ref, r_ref, o_ref):
    o_ref[...] = r_ref[...] + jnp.dot(a_ref[...], w_ref[...], preferred_element_type=F32)


def _matmul_res(a, w, res, *, name):
    m, k = a.shape
    n = w.shape[1]
    tm, tn = _tile(m, TM), _tile(n, TN, LANES)
    return pl.pallas_call(
        _matmul_res_kernel,
        grid=(m // tm, n // tn),
        in_specs=[
            pl.BlockSpec((tm, k), lambda i, j: (i, 0)),
            pl.BlockSpec((k, tn), lambda i, j: (0, j)),
            pl.BlockSpec((tm, tn), lambda i, j: (i, j)),
        ],
        out_specs=pl.BlockSpec((tm, tn), lambda i, j: (i, j)),
        out_shape=jax.ShapeDtypeStruct((m, n), F32),
        compiler_params=_params("parallel", "parallel"),
        name=name,
    )(a, w, res)


def _s5_out_kernel(y_ref, wv_ref, wg_ref, r_ref, o_ref, h_ref):
    @pl.when(pl.program_id(1) == 0)
    def _():
        h_ref[...] = _gelu_tanh(y_ref[...].astype(F32)).astype(BF16)

    h = h_ref[...]
    val = jnp.dot(h, wv_ref[...], preferred_element_type=F32)
    gate = jnp.dot(h, wg_ref[...], preferred_element_type=F32)
    o_ref[...] = r_ref[...] + val * _sigmoid(gate)


def _s5_out(y, w_out, res):
    m, k = y.shape
    n = w_out.shape[1] // 2
    tm, tn = _tile(m, TM), _tile(n, TN, LANES)
    nb = n // tn
    return pl.pallas_call(
        _s5_out_kernel,
        grid=(m // tm, nb),
        in_specs=[
            pl.BlockSpec((tm, k), lambda i, j: (i, 0)),
            pl.BlockSpec((k, tn), lambda i, j: (0, j)),
            pl.BlockSpec((k, tn), lambda i, j: (0, j + nb)),
            pl.BlockSpec((tm, tn), lambda i, j: (i, j)),
        ],
        out_specs=pl.BlockSpec((tm, tn), lambda i, j: (i, j)),
        out_shape=jax.ShapeDtypeStruct((m, n), F32),
        scratch_shapes=[pltpu.VMEM((tm, k), BF16)],
        compiler_params=_params("parallel", "arbitrary"),
        name="s5_out_glu",
    )(y, w_out, w_out, res)


def _swiglu_step(h, wg, wu, wd):
    g = jnp.dot(h, wg, preferred_element_type=F32)
    u = jnp.dot(h, wu, preferred_element_type=F32)
    a = (g * _sigmoid(g) * u).astype(BF16)
    return jnp.dot(a, wd, preferred_element_type=F32)


def _ffn_kernel(x_ref, g_ref, wg_ref, wu_ref, wd_ref, o_ref, h_ref, acc_ref):
    f = pl.program_id(1)

    @pl.when(f == 0)
    def _():
        h_ref[...] = _rms(x_ref[...], g_ref[...]).astype(BF16)
        acc_ref[...] = jnp.zeros_like(acc_ref)

    acc_ref[...] += _swiglu_step(h_ref[...], wg_ref[...], wu_ref[...], wd_ref[...])

    @pl.when(f == pl.num_programs(1) - 1)
    def _():
        o_ref[...] = x_ref[...] + acc_ref[...]


def _ffn(x, gain, w_gu, w_down, layer):
    m, d = x.shape
    ff = w_down.shape[1]
    tm, tf = _tile(m, TM), _tile(ff, TF, LANES)
    nf = ff // tf
    return pl.pallas_call(
        _ffn_kernel,
        grid=(m // tm, nf),
        in_specs=[
            pl.BlockSpec((tm, d), lambda i, f: (i, 0)),
            pl.BlockSpec((1, d), lambda i, f: (0, 0)),
            pl.BlockSpec((None, d, tf), lambda i, f: (layer, 0, f)),
            pl.BlockSpec((None, d, tf), lambda i, f: (layer, 0, f + nf)),
            pl.BlockSpec((None, tf, d), lambda i, f: (layer, f, 0)),
        ],
        out_specs=pl.BlockSpec((tm, d), lambda i, f: (i, 0)),
        out_shape=jax.ShapeDtypeStruct((m, d), F32),
        scratch_shapes=[pltpu.VMEM((tm, d), BF16), pltpu.VMEM((tm, d), F32)],
        compiler_params=_params("parallel", "arbitrary"),
        name="ffn_swiglu",
    )(x, gain.reshape(1, d).astype(F32), w_gu, w_gu, w_down)


def _rope_table(seq):
    half = MLA_ROPE // 2
    inv = ROPE_THETA ** (-jnp.arange(half, dtype=F32) / half)
    ang = jnp.arange(seq, dtype=F32)[:, None] * inv
    cos, sin = jnp.cos(ang), jnp.sin(ang)
    return jnp.concatenate([cos, cos, sin, sin], axis=-1)


def _rot_cols(w):
    half = w.shape[-1] // 2
    return jnp.concatenate([-w[..., half:], w[..., :half]], axis=-1)


def _rope_slab(slab, cs):
    prod = slab * cs
    return prod + pltpu.roll(prod, MLA_ROPE, axis=1)


def _krope_kernel(p_ref, cs_ref, o_ref):
    r = _rope_slab(p_ref[...].astype(F32), cs_ref[...])
    lane = lax.broadcasted_iota(jnp.int32, r.shape, 1)
    o_ref[...] = jnp.where(lane < MLA_ROPE, r, 0.0).astype(o_ref.dtype)


def _krope(proj, cs, seq, col_block):
    t = proj.shape[0]
    tm = _tile(seq, TM)
    ns = seq // tm
    return pl.pallas_call(
        _krope_kernel,
        grid=(t // tm,),
        in_specs=[
            pl.BlockSpec((tm, LANES), lambda i: (i, col_block)),
            pl.BlockSpec((tm, LANES), lambda i: (i % ns, 0)),
        ],
        out_specs=pl.BlockSpec((tm, LANES), lambda i: (i, 0)),
        out_shape=jax.ShapeDtypeStruct((t, LANES), BF16),
        compiler_params=_params("parallel"),
        name="mla_k_rope",
    )(proj, cs)


def _attn_kernel(qi_ref, ki_ref, q_ref, cs_ref, kn_ref, kr_ref, v_ref, o_ref,
                 qc_ref, m_ref, acc_ref):
    s_id = pl.program_id(2)
    qi = qi_ref[s_id]
    ki = ki_ref[s_id]

    @pl.when(ki == 0)
    def _():
        q = q_ref[...]
        qr = _rope_slab(q[:, LANES:].astype(F32), cs_ref[...]).astype(BF16)
        qc_ref[...] = jnp.concatenate([q[:, :LANES], qr], axis=1)
        m_ref[...] = jnp.full_like(m_ref, -jnp.inf)
        acc_ref[...] = jnp.zeros_like(acc_ref)

    kc = jnp.concatenate([kn_ref[...], kr_ref[...]], axis=1)
    lane = lax.broadcasted_iota(jnp.int32, v_ref.shape, 1)
    ve = jnp.concatenate([v_ref[...], jnp.where(lane == 0, 1.0, 0.0).astype(BF16)], axis=1)
    rq = q_ref.shape[0] // ATT_GROUPS

    def scores(g):
        return lax.dot_general(qc_ref[g * rq:(g + 1) * rq, :], kc, (((1,), (1,)), ((), ())),
                               preferred_element_type=F32)

    def update(g, sc, masked):
        rows = slice(g * rq, (g + 1) * rq)
        if masked:
            row = lax.broadcasted_iota(jnp.int32, sc.shape, 0) + g * rq
            col = lax.broadcasted_iota(jnp.int32, sc.shape, 1)
            sc = jnp.where(col <= row, sc, -jnp.inf)
        m_prev = m_ref[rows, :]
        m_new = jnp.maximum(m_prev, jnp.max(sc, axis=-1, keepdims=True))
        alpha = jnp.exp2(m_prev - m_new)
        p = jnp.exp2(sc - m_new)
        acc_ref[rows, :] = alpha * acc_ref[rows, :] + jnp.dot(p.astype(BF16), ve, preferred_element_type=F32)
        m_ref[rows, :] = m_new

    def sweep(masked):
        nxt = scores(0)
        for g in range(ATT_GROUPS):
            cur = nxt
            if g + 1 < ATT_GROUPS:
                nxt = scores(g + 1)
            update(g, cur, masked)

    @pl.when(ki < qi)
    def _():
        sweep(False)

    @pl.when(ki == qi)
    def _():
        sweep(True)
        a = acc_ref[...]
        o_ref[...] = (a[:, :LANES] / a[:, LANES:LANES + 1]).astype(o_ref.dtype)


def _attention(q_ext, kv_ext, k_rope, cs, batch, seq):
    t = q_ext.shape[0]
    heads = q_ext.shape[1] // (2 * LANES)
    tq = _tile(seq, ATT_TILE)
    nq = seq // tq
    steps = [(i, j) for i in range(nq) for j in range(i + 1)]
    qi_tab = jnp.asarray([s[0] for s in steps], jnp.int32)
    ki_tab = jnp.asarray([s[1] for s in steps], jnp.int32)
    grid_spec = pltpu.PrefetchScalarGridSpec(
        num_scalar_prefetch=2,
        grid=(batch, heads, len(steps)),
        in_specs=[
            pl.BlockSpec((tq, 2 * LANES), lambda b, h, s, qi, ki: (b * nq + qi[s], h)),
            pl.BlockSpec((tq, LANES), lambda b, h, s, qi, ki: (qi[s], 0)),
            pl.BlockSpec((tq, LANES), lambda b, h, s, qi, ki: (b * nq + ki[s], 2 * h)),
            pl.BlockSpec((tq, LANES), lambda b, h, s, qi, ki: (b * nq + ki[s], 0)),
            pl.BlockSpec((tq, LANES), lambda b, h, s, qi, ki: (b * nq + ki[s], 2 * h + 1)),
        ],
        out_specs=pl.BlockSpec((tq, LANES), lambda b, h, s, qi, ki: (b * nq + qi[s], h)),
        scratch_shapes=[
            pltpu.VMEM((tq, 2 * LANES), BF16),
            pltpu.VMEM((tq, 1), F32),
            pltpu.VMEM((tq, 2 * LANES), F32),
        ],
    )
    return pl.pallas_call(
        _attn_kernel,
        grid_spec=grid_spec,
        out_shape=jax.ShapeDtypeStruct((t, heads * LANES), BF16),
        compiler_params=_params("parallel", "parallel", "arbitrary"),
        name="mla_flash_attention",
    )(qi_tab, ki_tab, q_ext, cs, kv_ext, k_rope, kv_ext)


def _mla(x, ln, w_in, q_norm, kv_norm, w_uq, w_ukv, w_o, batch, seq):
    heads = MLA_HEADS
    assert MLA_NOPE == LANES and MLA_V == LANES and 2 * MLA_ROPE == LANES
    assert MLA_Q_RANK == MLA_KV_RANK
    rank = MLA_Q_RANK
    w_kr = w_in[:, 2 * rank:]
    w_in_ext = jnp.concatenate([w_in, _rot_cols(w_kr)], axis=1).astype(BF16)
    proj = _pro_matmul(x, ln, w_in_ext, prologue="rms", out_dtype=F32, name="mla_in_proj")
    scale = (MLA_NOPE + MLA_ROPE) ** -0.5 * math.log2(math.e)
    wq = w_uq.reshape(rank, heads, MLA_NOPE + MLA_ROPE) * scale
    wq_ext = jnp.concatenate([wq, _rot_cols(wq[..., MLA_NOPE:])], axis=-1)
    wq_ext = wq_ext.reshape(rank, heads * 2 * LANES).astype(BF16)
    q_ext = _pro_matmul(proj, q_norm, wq_ext, prologue="rms", out_dtype=BF16, x_col_block=0, name="mla_q_up")
    kv_ext = _pro_matmul(proj, kv_norm, w_ukv.astype(BF16), prologue="rms", out_dtype=BF16, x_col_block=1,
                         name="mla_kv_up")
    cs = _rope_table(seq)
    k_rope = _krope(proj, cs, seq, col_block=2 * rank // LANES)
    o = _attention(q_ext, kv_ext, k_rope, cs, batch, seq)
    return _matmul_res(o, w_o.astype(BF16), x, name="mla_out_proj")


def _hg_tables(c):
    levels = int(math.log2(c))
    assert 2 ** levels == c
    t = np.arange(c)[:, None]
    u = np.arange(c)[None, :]
    mats = []
    for l in range(1, levels + 1):
        mid = ((t >> l) << l) + (1 << (l - 1))
        upper = t >= mid
        mats.append(np.where(upper, (u >= mid) & (u <= t), (u > t) & (u <= mid - 1)))
    mats.append(u <= t)
    mats.append(u > t)
    masks = [t == u]
    for l in range(1, levels + 1):
        masks.append(((t >> l) == (u >> l)) & (((t >> (l - 1)) & 1) == 1) & (((u >> (l - 1)) & 1) == 0))
    return (np.concatenate(mats, axis=0).astype(np.float32), np.concatenate(masks, axis=0).astype(np.float32),
            levels)


def _split3(x):
    hi = x.astype(BF16)
    r = x - hi.astype(F32)
    mid = r.astype(BF16)
    lo = (r - mid.astype(F32)).astype(BF16)
    return hi, mid, lo


def _hgrn2_kernel(q_ref, z_ref, v_ref, g_ref, lb_ref, gn_ref, tab_ref, msk_ref, o_ref, st_ref, *, chunk, levels):
    c = chunk
    n_chunks = q_ref.shape[0] // c
    dk = q_ref.shape[1]

    @pl.when(pl.program_id(2) == 0)
    def _():
        st_ref[...] = jnp.zeros_like(st_ref)

    lb = lb_ref[...]
    log_lb = jnp.log(lb)
    log_1mlb = jnp.log1p(-lb)
    z = z_ref[...]
    q = q_ref[...].astype(F32)
    e = jnp.exp(-jnp.abs(z))
    log_sig = jnp.minimum(z, 0.0) - jnp.log1p(e)
    bterm = log_1mlb + log_sig
    lf = jnp.maximum(log_lb, bterm) + jnp.log1p(jnp.exp(-jnp.abs(log_lb - bterm)))
    k = (1.0 - lb) * jnp.where(z >= 0, e, 1.0) / (1.0 + e)

    def chunks_on_lanes(x):
        return jnp.concatenate([x[i * c:(i + 1) * c] for i in range(n_chunks)], axis=1)

    tab = tab_ref[...]
    p0, p1, p2 = _split3(lf)
    sums = (jnp.dot(tab, chunks_on_lanes(p0), preferred_element_type=F32)
            + jnp.dot(tab, chunks_on_lanes(p1), preferred_element_type=F32)
            + jnp.dot(tab, chunks_on_lanes(p2), preferred_element_type=F32))
    dec = jnp.exp(sums)
    diag = jnp.sum(q * k, axis=-1, keepdims=True)
    o_intra, q_in, upds, d_last = [], [], [], []
    for i in range(n_chunks):
        rows = slice(i * c, (i + 1) * c)
        qi, ki, vi = q[rows], k[rows], v_ref[rows, :]
        di = dec[:, i * dk:(i + 1) * dk]
        scores = msk_ref[0:c, :] * diag[rows]
        for l in range(1, levels + 1):
            d = di[(l - 1) * c:l * c]
            sl_ = lax.dot_general((qi * d).astype(BF16), (ki * d).astype(BF16), (((1,), (1,)), ((), ())),
                                  preferred_element_type=F32)
            scores = scores + msk_ref[l * c:(l + 1) * c, :] * sl_
        d_pre = di[levels * c:(levels + 1) * c]
        d_suf = di[(levels + 1) * c:(levels + 2) * c]
        o_intra.append(jnp.dot(scores.astype(BF16), vi, preferred_element_type=F32))
        q_in.append((qi * d_pre).astype(BF16))
        upds.append(lax.dot_general(vi, (ki * d_suf).astype(BF16), (((0,), (0,)), ((), ())),
                                    preferred_element_type=F32))
        d_last.append(d_pre[c - 1:c, :])
    st = st_ref[...]
    outs = []
    for i in range(n_chunks):
        outs.append(o_intra[i] + lax.dot_general(q_in[i], st.astype(BF16), (((1,), (1,)), ((), ())),
                                                 preferred_element_type=F32))
        st = st * d_last[i] + upds[i]
    st_ref[...] = st
    o = jnp.concatenate(outs, axis=0)
    gt = g_ref[...].astype(F32)
    o_ref[...] = (_rms(o, gn_ref[...]) * (gt * _sigmoid(gt))).astype(o_ref.dtype)


def _hgrn2_core(qvg, z, lb, g_norm, batch, seq):
    t = z.shape[0]
    heads = z.shape[1] // LANES
    ts = _tile(seq, HG_TILE)
    ns = seq // ts
    c = min(HG_CHUNK, ts)
    tab, msk, levels = _hg_tables(c)
    tab = jnp.asarray(tab, BF16)
    msk = jnp.asarray(msk, F32)
    return pl.pallas_call(
        functools.partial(_hgrn2_kernel, chunk=c, levels=levels),
        grid=(batch, heads, ns),
        in_specs=[
            pl.BlockSpec((ts, LANES), lambda b, h, s: (b * ns + s, h)),
            pl.BlockSpec((ts, LANES), lambda b, h, s: (b * ns + s, h)),
            pl.BlockSpec((ts, LANES), lambda b, h, s: (b * ns + s, heads + h)),
            pl.BlockSpec((ts, LANES), lambda b, h, s: (b * ns + s, 2 * heads + h)),
            pl.BlockSpec((1, LANES), lambda b, h, s: (0, h)),
            pl.BlockSpec((1, LANES), lambda b, h, s: (0, 0)),
            pl.BlockSpec(tab.shape, lambda b, h, s: (0, 0)),
            pl.BlockSpec(msk.shape, lambda b, h, s: (0, 0)),
        ],
        out_specs=pl.BlockSpec((ts, LANES), lambda b, h, s: (b * ns + s, h)),
        out_shape=jax.ShapeDtypeStruct((t, heads * LANES), BF16),
        scratch_shapes=[pltpu.VMEM((LANES, LANES), F32)],
        compiler_params=_params("parallel", "parallel", "arbitrary"),
        name="hgrn2_recurrence",
    )(qvg, z, qvg, qvg, lb.reshape(1, -1).astype(F32), g_norm.reshape(1, -1).astype(F32), tab, msk)


def _hgrn2(x, ln, w_in, lb, g_norm, w_o, batch, seq):
    d = x.shape[1]
    assert d == HG_HEADS * LANES
    w_qvg = jnp.concatenate([w_in[:, :d], w_in[:, 2 * d:]], axis=1).astype(BF16)
    w_z = w_in[:, d:2 * d].astype(BF16)
    qvg = _pro_matmul(x, ln, w_qvg, prologue="rms", out_dtype=BF16, name="hg_in_proj_qvg")
    z = _pro_matmul(x, ln, w_z, prologue="rms", out_dtype=F32, name="hg_in_proj_z")
    o = _hgrn2_core(qvg, z, lb, g_norm, batch, seq)
    return _matmul_res(o, w_o.astype(BF16), x, name="hg_out_proj")


def _s5_tables(a_re, a_im, b_re, b_im, c_re, c_im, d_skip, log_dt, nblk):
    L = S5_BLOCK
    G, P = a_re.shape
    N = b_re.shape[-1]
    a = lax.complex(a_re.astype(F32), a_im.astype(F32))
    dt_a = a * jnp.exp(log_dt.astype(F32))[:, None]
    a_bar = jnp.exp(dt_a)
    b_bar = ((a_bar - 1.0) / a)[:, :, None] * lax.complex(b_re.astype(F32), b_im.astype(F32))
    cc = lax.complex(c_re.astype(F32), c_im.astype(F32))
    pw = jnp.exp(jnp.arange(L + 1, dtype=F32)[:, None, None] * dt_a)
    kern = jnp.real(jnp.einsum("gnp,tgp,gpm->gtnm", cc, pw[:L], b_bar))
    kern = kern.at[:, 0].add(jax.vmap(jnp.diag)(d_skip.astype(F32)))
    s_idx = jnp.arange(L)[:, None]
    t_idx = jnp.arange(L)[None, :]
    lag = t_idx - s_idx
    toep = jnp.where((lag >= 0)[None, :, :, None, None], kern[:, jnp.clip(lag, 0, L - 1)], 0.0)
    toep = toep.transpose(0, 1, 4, 2, 3).reshape(G, L * N, L * N)
    wst = jnp.einsum("sgp,gpm->gsmp", pw[:L][::-1], b_bar)
    w_st = jnp.concatenate([jnp.real(wst), jnp.imag(wst)], axis=-1).reshape(G, L * N, 2 * P)
    co = jnp.einsum("gnp,tgp->gptn", cc, pw[1:])
    c_out = jnp.concatenate([jnp.real(co), -jnp.imag(co)], axis=1).reshape(G, 2 * P, L * N)
    nsteps = max(1, int(math.ceil(math.log2(nblk))))
    lam_k = jnp.exp((L * 2.0 ** jnp.arange(nsteps, dtype=F32))[None, :, None] * dt_a[:, None, :])
    lr, li = jnp.real(lam_k), jnp.imag(lam_k)
    lam = jnp.stack([jnp.concatenate([lr, lr], -1), jnp.concatenate([-li, li], -1)], axis=2)
    return toep.astype(BF16), w_st.astype(BF16), c_out.astype(BF16), lam.astype(F32)


def _s5_kernel(u_ref, toep_ref, wst_ref, cout_ref, lam_ref, y_ref, *, nsteps):
    L, N = S5_BLOCK, S5_GROUP
    nblk = u_ref.shape[0] // L
    gpl = LANES // N
    nchunk = L // gpl
    lane_grp = lax.broadcasted_iota(jnp.int32, (1, LANES), 1) // N

    def place(pieces, shifts):
        out = None
        for i, (p, s) in enumerate(zip(pieces, shifts)):
            r = p if s % gpl == 0 else pltpu.roll(p, (N * s) % LANES, axis=1)
            out = r if out is None else jnp.where(lane_grp == i, r, out)
        return out

    v = [u_ref[pl.ds(t, nblk, stride=L), :] for t in range(L)]
    blk = lax.broadcasted_iota(jnp.int32, (nblk, wst_ref.shape[-1]), 0)
    ys = []
    for g in range(gpl):
        ug = jnp.concatenate(
            [place([v[j * gpl + tau] for tau in range(gpl)], [tau - g for tau in range(gpl)])
             for j in range(nchunk)], axis=1).astype(BF16)
        z = jnp.dot(ug, wst_ref[g], preferred_element_type=F32)
        p2 = z.shape[1]
        h = z
        for k in range(nsteps):
            sh = 1 << k
            prev = jnp.where(blk >= sh, pltpu.roll(h, sh, axis=0), 0.0)
            lam = lam_ref[g, k]
            h = h + prev * lam[0:1, :] + pltpu.roll(prev, p2 // 2, axis=1) * lam[1:2, :]
        h0 = jnp.where(blk >= 1, pltpu.roll(h, 1, axis=0), 0.0)
        ys.append(jnp.dot(ug, toep_ref[g], preferred_element_type=F32)
                  + jnp.dot(h0.astype(BF16), cout_ref[g], preferred_element_type=F32))
    for t in range(L):
        j, tau = divmod(t, gpl)
        w = place([ys[g][:, j * LANES:(j + 1) * LANES] for g in range(gpl)], [g - tau for g in range(gpl)])
        y_ref[pl.ds(t, nblk, stride=L), :] = w


def _s5_core(u, toep, w_st, c_out, lam, batch, seq):
    t, d = u.shape
    gpl = LANES // S5_GROUP
    ln = toep.shape[-1]
    p2 = w_st.shape[-1]
    nsteps = lam.shape[1]
    return pl.pallas_call(
        functools.partial(_s5_kernel, nsteps=nsteps),
        grid=(d // LANES, batch),
        in_specs=[
            pl.BlockSpec((seq, LANES), lambda i, b: (b, i)),
            pl.BlockSpec((gpl, ln, ln), lambda i, b: (i, 0, 0)),
            pl.BlockSpec((gpl, ln, p2), lambda i, b: (i, 0, 0)),
            pl.BlockSpec((gpl, p2, ln), lambda i, b: (i, 0, 0)),
            pl.BlockSpec((gpl, nsteps, 2, p2), lambda i, b: (i, 0, 0, 0)),
        ],
        out_specs=pl.BlockSpec((seq, LANES), lambda i, b: (b, i)),
        out_shape=jax.ShapeDtypeStruct((t, d), F32),
        compiler_params=_params("parallel", "parallel"),
        name="s5_ssm",
    )(u, toep, w_st, c_out, lam)


def _s5(x, ln, w_in, a_re, a_im, b_re, b_im, c_re, c_im, d_skip, log_dt, w_out, batch, seq):
    nblk = seq // S5_BLOCK
    u = _pro_matmul(x, ln, w_in.astype(BF16), prologue="rms", out_dtype=F32, name="s5_in_proj")
    toep, w_st, c_out, lam = _s5_tables(a_re, a_im, b_re, b_im, c_re, c_im, d_skip, log_dt, nblk)
    y = _s5_core(u, toep, w_st, c_out, lam, batch, seq)
    return _s5_out(y, w_out.astype(BF16), x)


def _router_kernel(x_ref, g_ref, w_ref, h_ref, r_ref):
    h = _rms(x_ref[...], g_ref[...])
    h_ref[...] = h
    h0 = h.astype(BF16)
    h1 = (h - h0.astype(F32)).astype(BF16)
    w = w_ref[...]
    w0 = w.astype(BF16)
    w1 = (w - w0.astype(F32)).astype(BF16)
    logits = (jnp.dot(h0, w0, preferred_element_type=F32) + jnp.dot(h0, w1, preferred_element_type=F32)
              + jnp.dot(h1, w0, preferred_element_type=F32))
    lane = lax.broadcasted_iota(jnp.int32, logits.shape, 1).astype(F32)
    neg = -jnp.inf
    lg = jnp.where(lane < N_EXPERTS, logits, neg)
    m1 = jnp.max(lg, axis=-1, keepdims=True)
    i1 = jnp.min(jnp.where(lg == m1, lane, float(LANES)), axis=-1, keepdims=True)
    lg2 = jnp.where(lane == i1, neg, lg)
    m2 = jnp.max(lg2, axis=-1, keepdims=True)
    i2 = jnp.min(jnp.where(lg2 == m2, lane, float(LANES)), axis=-1, keepdims=True)
    e2 = jnp.exp(m2 - m1)
    g1 = 1.0 / (1.0 + e2)
    g2 = e2 / (1.0 + e2)
    out = jnp.where(lane == 0, g1, jnp.where(lane == 1, g2, jnp.where(lane == 2, i1, jnp.where(lane == 3, i2, 0.0))))
    r_ref[...] = out[:, :r_ref.shape[1]]


def _router(x, gain, w_router):
    m, d = x.shape
    tm = _tile(m, TM)
    w_pad = jnp.zeros((d, LANES), F32).at[:, :N_EXPERTS].set(w_router.astype(F32))
    return pl.pallas_call(
        _router_kernel,
        grid=(m // tm,),
        in_specs=[
            pl.BlockSpec((tm, d), lambda i: (i, 0)),
            pl.BlockSpec((1, d), lambda i: (0, 0)),
            pl.BlockSpec((d, LANES), lambda i: (0, 0)),
        ],
        out_specs=[pl.BlockSpec((tm, d), lambda i: (i, 0)), pl.BlockSpec((tm, 8), lambda i: (i, 0))],
        out_shape=[jax.ShapeDtypeStruct((m, d), F32), jax.ShapeDtypeStruct((m, 8), F32)],
        compiler_params=_params("parallel"),
        name="moe_router_top2",
    )(x, gain.reshape(1, d).astype(F32), w_pad)


def _moe_kernel(te_ref, ta_ref, idx_hbm, h_hbm, wg_ref, wu_ref, wd_ref, y_hbm,
                idx_ref, x_ref, xb_ref, acc_ref, sem_i, sem_g, sem_s):
    i = pl.program_id(0)
    f = pl.program_id(1)
    tm = x_ref.shape[0]
    active = ta_ref[i] == 1

    def gather_copy(r, tok):
        return pltpu.make_async_copy(h_hbm.at[pl.ds(tok, 1)], x_ref.at[pl.ds(r, 1)], sem_g)

    def scatter_copy(r, dst):
        return pltpu.make_async_copy(acc_ref.at[pl.ds(r, 1)], y_hbm.at[pl.ds(dst, 1)], sem_s)

    @pl.when(active & (f == 0))
    def _():
        cp = pltpu.make_async_copy(idx_hbm.at[i], idx_ref, sem_i)
        cp.start()
        cp.wait()

        def issue(r, c):
            gather_copy(r, idx_ref[0, r]).start()
            return c

        lax.fori_loop(0, tm, issue, 0, unroll=DMA_UNROLL)

        def drain(r, c):
            gather_copy(r, 0).wait()
            return c

        lax.fori_loop(0, tm, drain, 0, unroll=DMA_UNROLL)
        xb_ref[...] = x_ref[...].astype(BF16)
        acc_ref[...] = jnp.zeros_like(acc_ref)

    @pl.when(active)
    def _():
        acc_ref[...] += _swiglu_step(xb_ref[...], wg_ref[...], wu_ref[...], wd_ref[...])

    @pl.when(active & (f == pl.num_programs(1) - 1))
    def _():
        def issue(r, c):
            dst = idx_ref[1, r]

            @pl.when(dst >= 0)
            def _():
                scatter_copy(r, dst).start()

            return c

        lax.fori_loop(0, tm, issue, 0, unroll=DMA_UNROLL)

        def drain(r, c):
            @pl.when(idx_ref[1, r] >= 0)
            def _():
                scatter_copy(r, 0).wait()

            return c

        lax.fori_loop(0, tm, drain, 0, unroll=DMA_UNROLL)


def _moe_experts(h, idx, tile_expert, tile_active, w_gu, w_down, layer):
    n_tok, d = h.shape
    n_tiles, _, tm = idx.shape
    ff = w_down.shape[2]
    tf = _tile(ff, TF, LANES)
    nf = ff // tf

    def fe(f, i, ta):
        return jnp.where(ta[i] == 1, f, nf - 1)

    grid_spec = pltpu.PrefetchScalarGridSpec(
        num_scalar_prefetch=2,
        grid=(n_tiles, nf),
        in_specs=[
            pl.BlockSpec(memory_space=pl.ANY),
            pl.BlockSpec(memory_space=pl.ANY),
            pl.BlockSpec((None, None, d, tf), lambda i, f, te, ta: (layer, te[i], 0, fe(f, i, ta))),
            pl.BlockSpec((None, None, d, tf), lambda i, f, te, ta: (layer, te[i], 0, fe(f, i, ta) + nf)),
            pl.BlockSpec((None, None, tf, d), lambda i, f, te, ta: (layer, te[i], fe(f, i, ta), 0)),
        ],
        out_specs=pl.BlockSpec(memory_space=pl.ANY),
        scratch_shapes=[
            pltpu.SMEM((2, tm), jnp.int32),
            pltpu.VMEM((tm, d), F32),
            pltpu.VMEM((tm, d), BF16),
            pltpu.VMEM((tm, d), F32),
            pltpu.SemaphoreType.DMA,
            pltpu.SemaphoreType.DMA,
            pltpu.SemaphoreType.DMA,
        ],
    )
    return pl.pallas_call(
        _moe_kernel,
        grid_spec=grid_spec,
        out_shape=jax.ShapeDtypeStruct((TOP_K * n_tok, d), F32),
        compiler_params=_params("arbitrary", "arbitrary"),
        name="moe_grouped_swiglu",
    )(tile_expert, tile_active, idx, h, w_gu, w_gu, w_down)


def _combine_kernel(x_ref, a_ref, b_ref, r_ref, o_ref):
    r = r_ref[...]
    o_ref[...] = x_ref[...] + (r[:, 0:1] * a_ref[...] + r[:, 1:2] * b_ref[...])


def _combine(x, y, route):
    m, d = x.shape
    tm = _tile(m, TM)
    nb = m // tm
    return pl.pallas_call(
        _combine_kernel,
        grid=(nb,),
        in_specs=[
            pl.BlockSpec((tm, d), lambda i: (i, 0)),
            pl.BlockSpec((tm, d), lambda i: (i, 0)),
            pl.BlockSpec((tm, d), lambda i: (i + nb, 0)),
            pl.BlockSpec((tm, route.shape[1]), lambda i: (i, 0)),
        ],
        out_specs=pl.BlockSpec((tm, d), lambda i: (i, 0)),
        out_shape=jax.ShapeDtypeStruct((m, d), F32),
        compiler_params=_params("parallel"),
        name="moe_combine",
    )(x, y, y, route)


def _route_metadata(route, n_tok, tm):
    experts = route[:, TOP_K:2 * TOP_K].T.reshape(-1).astype(jnp.int32)
    n_pairs = TOP_K * n_tok
    onehot = (experts[:, None] == jnp.arange(N_EXPERTS)[None, :]).astype(jnp.int32)
    rank = jnp.take_along_axis(jnp.cumsum(onehot, axis=0), experts[:, None], axis=1)[:, 0] - 1
    counts = jnp.sum(onehot, axis=0)
    padded = ((counts + tm - 1) // tm) * tm
    ends = jnp.cumsum(padded)
    starts = ends - padded
    rows = starts[experts] + rank
    n_rows = n_pairs + N_EXPERTS * tm
    n_rows = (n_rows // tm) * tm
    dest = jnp.full((n_rows,), -1, jnp.int32).at[rows].set(jnp.arange(n_pairs, dtype=jnp.int32))
    src = jnp.where(dest < 0, 0, dest % n_tok)
    idx = jnp.stack([src.reshape(-1, tm), dest.reshape(-1, tm)], axis=1)
    tile_start = jnp.arange(n_rows // tm, dtype=jnp.int32) * tm
    tile_expert = jnp.sum((tile_start[:, None] >= ends[None, :]).astype(jnp.int32), axis=1)
    tile_expert = jnp.minimum(tile_expert, N_EXPERTS - 1)
    tile_active = (tile_start < ends[-1]).astype(jnp.int32)
    return idx, tile_expert, tile_active


def _moe(x, gain, w_router, w_gu, w_down, layer):
    n_tok = x.shape[0]
    h, route = _router(x, gain, w_router)
    tm = min(MOE_TM, n_tok)
    idx, tile_expert, tile_active = _route_metadata(route, n_tok, tm)
    y = _moe_experts(h, idx, tile_expert, tile_active, w_gu, w_down, layer)
    return _combine(x, y, route)


def _final_norm_kernel(x_ref, g_ref, o_ref):
    o_ref[...] = _rms(x_ref[...], g_ref[...])


def _final_norm(x, gain):
    m, d = x.shape
    tm = _tile(m, TM)
    return pl.pallas_call(
        _final_norm_kernel,
        grid=(m // tm,),
        in_specs=[pl.BlockSpec((tm, d), lambda i: (i, 0)), pl.BlockSpec((1, d), lambda i: (0, 0))],
        out_specs=pl.BlockSpec((tm, d), lambda i: (i, 0)),
        out_shape=jax.ShapeDtypeStruct((m, d), F32),
        compiler_params=_params("parallel"),
        name="final_rmsnorm",
    )(x, gain.reshape(1, d).astype(F32))


def kernel(x, ln_mix, ln_ffn, ln_final, mla_w_in, mla_q_norm, mla_kv_norm, mla_w_uq, mla_w_ukv, mla_w_o, hg_w_in, hg_lower_bound, hg_g_norm, hg_w_o, s5_w_in, s5_a_re, s5_a_im, s5_b_re, s5_b_im, s5_c_re, s5_c_im, s5_d, s5_log_dt, s5_w_out, ffn_w_gu, ffn_w_down, moe_w_router, moe_w_gu, moe_w_down):
    batch, seq, d = x.shape
    depth = ln_mix.shape[0]
    lb_w = jax.nn.softmax(hg_lower_bound.astype(F32), axis=0)
    lower_bounds = jnp.cumsum(lb_w, axis=0) - lb_w[0]
    xt = x.reshape(batch * seq, d).astype(F32)
    ffn_gu, ffn_down = ffn_w_gu.astype(BF16), ffn_w_down.astype(BF16)
    moe_gu, moe_down = moe_w_gu.astype(BF16), moe_w_down.astype(BF16)
    for i in range(depth):
        m, j = i % N_MIXERS, i // N_MIXERS
        if m == 0:
            xt = _mla(xt, ln_mix[i], mla_w_in[j], mla_q_norm[j], mla_kv_norm[j], mla_w_uq[j], mla_w_ukv[j],
                      mla_w_o[j], batch, seq)
        elif m == 1:
            xt = _hgrn2(xt, ln_mix[i], hg_w_in[j], lower_bounds[i], hg_g_norm[j], hg_w_o[j], batch, seq)
        else:
            xt = _s5(xt, ln_mix[i], s5_w_in[j], s5_a_re[j], s5_a_im[j], s5_b_re[j], s5_b_im[j], s5_c_re[j],
                     s5_c_im[j], s5_d[j], s5_log_dt[j], s5_w_out[j], batch, seq)
        f = i // 2
        if i % 2 == 0:
            xt = _ffn(xt, ln_ffn[i], ffn_gu, ffn_down, f)
        else:
            xt = _moe(xt, ln_ffn[i], moe_w_router[f], moe_gu, moe_down, f)
    return _final_norm(xt, ln_final).reshape(batch, seq, d)
```

```python
import functools
import math

import numpy as np
import jax
import jax.numpy as jnp
from jax import lax
from jax.experimental import pallas as pl
from jax.experimental.pallas import tpu as pltpu

F32 = jnp.float32
BF16 = jnp.bfloat16
EPS = 1e-6
N_MIXERS = 3

MLA_HEADS = 16
MLA_Q_RANK = 512
MLA_KV_RANK = 512
MLA_NOPE = 128
MLA_ROPE = 64
MLA_V = 128
ROPE_THETA = 10000.0
HG_HEADS = 16
HG_CHUNK = 64
S5_GROUP = 16
S5_STATE = 64
S5_BLOCK = 32
N_EXPERTS = 8
TOP_K = 2

LANES = 128
VMEM_LIMIT = 56 * 1024 * 1024

TM = 512
TN = 1024
TF = 512
ATT_TILE = 1024
ATT_GROUPS = 4
HG_TILE = 512
MOE_TM = 512
DMA_UNROLL = 8
GATHER_CHUNK = 64


def _tile(n, t, step=8):
    if n % step:
        return n
    best = step
    for c in range(step, min(n, t) + 1, step):
        if n % c == 0:
            best = c
    return best


def _params(*sem):
    return pltpu.CompilerParams(dimension_semantics=sem, vmem_limit_bytes=VMEM_LIMIT)


def _rms(x, g):
    return x * lax.rsqrt(jnp.mean(x * x, axis=-1, keepdims=True) + EPS) * g


def _gelu_tanh(y):
    c = math.sqrt(2.0 / math.pi)
    return 0.5 * y * (1.0 + jnp.tanh(c * (y + 0.044715 * (y * y * y))))


def _sigmoid(x):
    return 1.0 / (1.0 + jnp.exp(-x))


def _pro_matmul_kernel(x_ref, g_ref, w_ref, o_ref, h_ref, *, prologue):
    @pl.when(pl.program_id(1) == 0)
    def _():
        x = x_ref[...].astype(F32)
        if prologue == "rms":
            x = _rms(x, g_ref[...])
        h_ref[...] = x.astype(BF16)

    o_ref[...] = jnp.dot(h_ref[...], w_ref[...], preferred_element_type=F32).astype(o_ref.dtype)


def _pro_matmul(x, gain, w, *, prologue, out_dtype, x_col_block=0, name):
    m = x.shape[0]
    k, n = w.shape
    tm, tn = _tile(m, TM * max(1, 1024 // k)), _tile(n, TN, LANES)
    return pl.pallas_call(
        functools.partial(_pro_matmul_kernel, prologue=prologue),
        grid=(m // tm, n // tn),
        in_specs=[
            pl.BlockSpec((tm, k), lambda i, j: (i, x_col_block)),
            pl.BlockSpec((1, k), lambda i, j: (0, 0)),
            pl.BlockSpec((k, tn), lambda i, j: (0, j)),
        ],
        out_specs=pl.BlockSpec((tm, tn), lambda i, j: (i, j)),
        out_shape=jax.ShapeDtypeStruct((m, n), out_dtype),
        scratch_shapes=[pltpu.VMEM((tm, k), BF16)],
        compiler_params=_params("parallel", "arbitrary"),
        name=name,
    )(x, gain.reshape(1, k).astype(F32), w)


def _matmul_res_kernel(a_ref, w_ref, r_ref, o_ref):
    o_ref[...] = r_ref[...] + jnp.dot(a_ref[...], w_ref[...], preferred_element_type=F32)


def _matmul_res(a, w, res, *, name):
    m, k = a.shape
    n = w.shape[1]
    tm, tn = _tile(m, TM), _tile(n, TN, LANES)
    return pl.pallas_call(
        _matmul_res_kernel,
        grid=(m // tm, n // tn),
        in_specs=[
            pl.BlockSpec((tm, k), lambda i, j: (i, 0)),
            pl.BlockSpec((k, tn), lambda i, j: (0, j)),
            pl.BlockSpec((tm, tn), lambda i, j: (i, j)),
        ],
        out_specs=pl.BlockSpec((tm, tn), lambda i, j: (i, j)),
        out_shape=jax.ShapeDtypeStruct((m, n), F32),
        compiler_params=_params("parallel", "parallel"),
        name=name,
    )(a, w, res)


def _s5_out_kernel(y_ref, wv_ref, wg_ref, r_ref, o_ref, h_ref):
    @pl.when(pl.program_id(1) == 0)
    def _():
        h_ref[...] = _gelu_tanh(y_ref[...].astype(F32)).astype(BF16)

    h = h_ref[...]
    val = jnp.dot(h, wv_ref[...], preferred_element_type=F32)
    gate = jnp.dot(h, wg_ref[...], preferred_element_type=F32)
    o_ref[...] = r_ref[...] + val * _sigmoid(gate)


def _s5_out(y, w_out, res):
    m, k = y.shape
    n = w_out.shape[1] // 2
    tm, tn = _tile(m, TM), _tile(n, TN, LANES)
    nb = n // tn
    return pl.pallas_call(
        _s5_out_kernel,
        grid=(m // tm, nb),
        in_specs=[
            pl.BlockSpec((tm, k), lambda i, j: (i, 0)),
            pl.BlockSpec((k, tn), lambda i, j: (0, j)),
            pl.BlockSpec((k, tn), lambda i, j: (0, j + nb)),
            pl.BlockSpec((tm, tn), lambda i, j: (i, j)),
        ],
        out_specs=pl.BlockSpec((tm, tn), lambda i, j: (i, j)),
        out_shape=jax.ShapeDtypeStruct((m, n), F32),
        scratch_shapes=[pltpu.VMEM((tm, k), BF16)],
        compiler_params=_params("parallel", "arbitrary"),
        name="s5_out_glu",
    )(y, w_out, w_out, res)


def _swiglu_step(h, wg, wu, wd):
    g = jnp.dot(h, wg, preferred_element_type=F32)
    u = jnp.dot(h, wu, preferred_element_type=F32)
    a = (g * _sigmoid(g) * u).astype(BF16)
    return jnp.dot(a, wd, preferred_element_type=F32)


def _ffn_kernel(x_ref, g_ref, wg_ref, wu_ref, wd_ref, o_ref, h_ref, acc_ref):
    f = pl.program_id(1)

    @pl.when(f == 0)
    def _():
        h_ref[...] = _rms(x_ref[...], g_ref[...]).astype(BF16)
        acc_ref[...] = jnp.zeros_like(acc_ref)

    acc_ref[...] += _swiglu_step(h_ref[...], wg_ref[...], wu_ref[...], wd_ref[...])

    @pl.when(f == pl.num_programs(1) - 1)
    def _():
        o_ref[...] = x_ref[...] + acc_ref[...]


def _ffn(x, gain, w_gu, w_down, layer):
    m, d = x.shape
    ff = w_down.shape[1]
    tm, tf = _tile(m, TM), _tile(ff, TF, LANES)
    nf = ff // tf
    return pl.pallas_call(
        _ffn_kernel,
        grid=(m // tm, nf),
        in_specs=[
            pl.BlockSpec((tm, d), lambda i, f: (i, 0)),
            pl.BlockSpec((1, d), lambda i, f: (0, 0)),
            pl.BlockSpec((None, d, tf), lambda i, f: (layer, 0, f)),
            pl.BlockSpec((None, d, tf), lambda i, f: (layer, 0, f + nf)),
            pl.BlockSpec((None, tf, d), lambda i, f: (layer, f, 0)),
        ],
        out_specs=pl.BlockSpec((tm, d), lambda i, f: (i, 0)),
        out_shape=jax.ShapeDtypeStruct((m, d), F32),
        scratch_shapes=[pltpu.VMEM((tm, d), BF16), pltpu.VMEM((tm, d), F32)],
        compiler_params=_params("parallel", "arbitrary"),
        name="ffn_swiglu",
    )(x, gain.reshape(1, d).astype(F32), w_gu, w_gu, w_down)


def _rope_table(seq):
    half = MLA_ROPE // 2
    inv = ROPE_THETA ** (-jnp.arange(half, dtype=F32) / half)
    ang = jnp.arange(seq, dtype=F32)[:, None] * inv
    cos, sin = jnp.cos(ang), jnp.sin(ang)
    return jnp.concatenate([cos, cos, sin, sin], axis=-1)


def _rot_cols(w):
    half = w.shape[-1] // 2
    return jnp.concatenate([-w[..., half:], w[..., :half]], axis=-1)


def _rope_slab(slab, cs):
    prod = slab * cs
    return prod + pltpu.roll(prod, MLA_ROPE, axis=1)


def _krope_kernel(p_ref, cs_ref, o_ref):
    r = _rope_slab(p_ref[...].astype(F32), cs_ref[...])
    lane = lax.broadcasted_iota(jnp.int32, r.shape, 1)
    o_ref[...] = jnp.where(lane < MLA_ROPE, r, 0.0).astype(o_ref.dtype)


def _krope(proj, cs, seq, col_block):
    t = proj.shape[0]
    tm = _tile(seq, TM)
    ns = seq // tm
    return pl.pallas_call(
        _krope_kernel,
        grid=(t // tm,),
        in_specs=[
            pl.BlockSpec((tm, LANES), lambda i: (i, col_block)),
            pl.BlockSpec((tm, LANES), lambda i: (i % ns, 0)),
        ],
        out_specs=pl.BlockSpec((tm, LANES), lambda i: (i, 0)),
        out_shape=jax.ShapeDtypeStruct((t, LANES), BF16),
        compiler_params=_params("parallel"),
        name="mla_k_rope",
    )(proj, cs)


def _attn_kernel(qi_ref, ki_ref, q_ref, cs_ref, kn_ref, kr_ref, v_ref, o_ref,
                 qc_ref, m_ref, acc_ref):
    s_id = pl.program_id(2)
    qi = qi_ref[s_id]
    ki = ki_ref[s_id]

    @pl.when(ki == 0)
    def _():
        q = q_ref[...]
        qr = _rope_slab(q[:, LANES:].astype(F32), cs_ref[...]).astype(BF16)
        qc_ref[...] = jnp.concatenate([q[:, :LANES], qr], axis=1)
        m_ref[...] = jnp.full_like(m_ref, -jnp.inf)
        acc_ref[...] = jnp.zeros_like(acc_ref)

    kc = jnp.concatenate([kn_ref[...], kr_ref[...]], axis=1)
    lane = lax.broadcasted_iota(jnp.int32, v_ref.shape, 1)
    ve = jnp.concatenate([v_ref[...], jnp.where(lane == 0, 1.0, 0.0).astype(BF16)], axis=1)
    rq = q_ref.shape[0] // ATT_GROUPS

    def scores(g):
        return lax.dot_general(qc_ref[g * rq:(g + 1) * rq, :], kc, (((1,), (1,)), ((), ())),
                               preferred_element_type=F32)

    def update(g, sc, masked):
        rows = slice(g * rq, (g + 1) * rq)
        if masked:
            row = lax.broadcasted_iota(jnp.int32, sc.shape, 0) + g * rq
            col = lax.broadcasted_iota(jnp.int32, sc.shape, 1)
            sc = jnp.where(col <= row, sc, -jnp.inf)
        m_prev = m_ref[rows, :]
        m_new = jnp.maximum(m_prev, jnp.max(sc, axis=-1, keepdims=True))
        alpha = jnp.exp2(m_prev - m_new)
        p = jnp.exp2(sc - m_new)
        acc_ref[rows, :] = alpha * acc_ref[rows, :] + jnp.dot(p.astype(BF16), ve, preferred_element_type=F32)
        m_ref[rows, :] = m_new

    def sweep(masked):
        nxt = scores(0)
        for g in range(ATT_GROUPS):
            cur = nxt
            if g + 1 < ATT_GROUPS:
                nxt = scores(g + 1)
            update(g, cur, masked)

    @pl.when(ki < qi)
    def _():
        sweep(False)

    @pl.when(ki == qi)
    def _():
        sweep(True)
        a = acc_ref[...]
        o_ref[...] = (a[:, :LANES] / a[:, LANES:LANES + 1]).astype(o_ref.dtype)


def _attention(q_ext, kv_ext, k_rope, cs, batch, seq):
    t = q_ext.shape[0]
    heads = q_ext.shape[1] // (2 * LANES)
    tq = _tile(seq, ATT_TILE)
    nq = seq // tq
    steps = [(i, j) for i in range(nq) for j in range(i + 1)]
    qi_tab = jnp.asarray([s[0] for s in steps], jnp.int32)
    ki_tab = jnp.asarray([s[1] for s in steps], jnp.int32)
    grid_spec = pltpu.PrefetchScalarGridSpec(
        num_scalar_prefetch=2,
        grid=(batch, heads, len(steps)),
        in_specs=[
            pl.BlockSpec((tq, 2 * LANES), lambda b, h, s, qi, ki: (b * nq + qi[s], h)),
            pl.BlockSpec((tq, LANES), lambda b, h, s, qi, ki: (qi[s], 0)),
            pl.BlockSpec((tq, LANES), lambda b, h, s, qi, ki: (b * nq + ki[s], 2 * h)),
            pl.BlockSpec((tq, LANES), lambda b, h, s, qi, ki: (b * nq + ki[s], 0)),
            pl.BlockSpec((tq, LANES), lambda b, h, s, qi, ki: (b * nq + ki[s], 2 * h + 1)),
        ],
        out_specs=pl.BlockSpec((tq, LANES), lambda b, h, s, qi, ki: (b * nq + qi[s], h)),
        scratch_shapes=[
            pltpu.VMEM((tq, 2 * LANES), BF16),
            pltpu.VMEM((tq, 1), F32),
            pltpu.VMEM((tq, 2 * LANES), F32),
        ],
    )
    return pl.pallas_call(
        _attn_kernel,
        grid_spec=grid_spec,
        out_shape=jax.ShapeDtypeStruct((t, heads * LANES), BF16),
        compiler_params=_params("parallel", "parallel", "arbitrary"),
        name="mla_flash_attention",
    )(qi_tab, ki_tab, q_ext, cs, kv_ext, k_rope, kv_ext)


def _mla(x, ln, w_in, q_norm, kv_norm, w_uq, w_ukv, w_o, batch, seq):
    heads = MLA_HEADS
    assert MLA_NOPE == LANES and MLA_V == LANES and 2 * MLA_ROPE == LANES
    assert MLA_Q_RANK == MLA_KV_RANK
    rank = MLA_Q_RANK
    w_kr = w_in[:, 2 * rank:]
    w_in_ext = jnp.concatenate([w_in, _rot_cols(w_kr)], axis=1).astype(BF16)
    proj = _pro_matmul(x, ln, w_in_ext, prologue="rms", out_dtype=F32, name="mla_in_proj")
    scale = (MLA_NOPE + MLA_ROPE) ** -0.5 * math.log2(math.e)
    wq = w_uq.reshape(rank, heads, MLA_NOPE + MLA_ROPE) * scale
    wq_ext = jnp.concatenate([wq, _rot_cols(wq[..., MLA_NOPE:])], axis=-1)
    wq_ext = wq_ext.reshape(rank, heads * 2 * LANES).astype(BF16)
    q_ext = _pro_matmul(proj, q_norm, wq_ext, prologue="rms", out_dtype=BF16, x_col_block=0, name="mla_q_up")
    kv_ext = _pro_matmul(proj, kv_norm, w_ukv.astype(BF16), prologue="rms", out_dtype=BF16, x_col_block=1,
                         name="mla_kv_up")
    cs = _rope_table(seq)
    k_rope = _krope(proj, cs, seq, col_block=2 * rank // LANES)
    o = _attention(q_ext, kv_ext, k_rope, cs, batch, seq)
    return _matmul_res(o, w_o.astype(BF16), x, name="mla_out_proj")


def _hg_tables(c):
    levels = int(math.log2(c))
    assert 2 ** levels == c
    t = np.arange(c)[:, None]
    u = np.arange(c)[None, :]
    mats = []
    for l in range(1, levels + 1):
        mid = ((t >> l) << l) + (1 << (l - 1))
        upper = t >= mid
        mats.append(np.where(upper, (u >= mid) & (u <= t), (u > t) & (u <= mid - 1)))
    mats.append(u <= t)
    mats.append(u > t)
    masks = [t == u]
    for l in range(1, levels + 1):
        masks.append(((t >> l) == (u >> l)) & (((t >> (l - 1)) & 1) == 1) & (((u >> (l - 1)) & 1) == 0))
    return (np.concatenate(mats, axis=0).astype(np.float32), np.concatenate(masks, axis=0).astype(np.float32),
            levels)


def _split3(x):
    hi = x.astype(BF16)
    r = x - hi.astype(F32)
    mid = r.astype(BF16)
    lo = (r - mid.astype(F32)).astype(BF16)
    return hi, mid, lo


def _hgrn2_kernel(q_ref, z_ref, v_ref, g_ref, lb_ref, gn_ref, tab_ref, msk_ref, o_ref, st_ref, *, chunk, levels):
    c = chunk
    n_chunks = q_ref.shape[0] // c
    dk = q_ref.shape[1]

    @pl.when(pl.program_id(2) == 0)
    def _():
        st_ref[...] = jnp.zeros_like(st_ref)

    lb = lb_ref[...]
    log_lb = jnp.log(lb)
    log_1mlb = jnp.log1p(-lb)
    z = z_ref[...]
    q = q_ref[...].astype(F32)
    e = jnp.exp(-jnp.abs(z))
    log_sig = jnp.minimum(z, 0.0) - jnp.log1p(e)
    bterm = log_1mlb + log_sig
    lf = jnp.maximum(log_lb, bterm) + jnp.log1p(jnp.exp(-jnp.abs(log_lb - bterm)))
    k = (1.0 - lb) * jnp.where(z >= 0, e, 1.0) / (1.0 + e)

    def chunks_on_lanes(x):
        return jnp.concatenate([x[i * c:(i + 1) * c] for i in range(n_chunks)], axis=1)

    tab = tab_ref[...]
    p0, p1, p2 = _split3(lf)
    sums = (jnp.dot(tab, chunks_on_lanes(p0), preferred_element_type=F32)
            + jnp.dot(tab, chunks_on_lanes(p1), preferred_element_type=F32)
            + jnp.dot(tab, chunks_on_lanes(p2), preferred_element_type=F32))
    dec = jnp.exp(sums)
    diag = jnp.sum(q * k, axis=-1, keepdims=True)
    o_intra, q_in, upds, d_last = [], [], [], []
    for i in range(n_chunks):
        rows = slice(i * c, (i + 1) * c)
        qi, ki, vi = q[rows], k[rows], v_ref[rows, :]
        di = dec[:, i * dk:(i + 1) * dk]
        scores = msk_ref[0:c, :] * diag[rows]
        for l in range(1, levels + 1):
            d = di[(l - 1) * c:l * c]
            sl_ = lax.dot_general((qi * d).astype(BF16), (ki * d).astype(BF16), (((1,), (1,)), ((), ())),
                                  preferred_element_type=F32)
            scores = scores + msk_ref[l * c:(l + 1) * c, :] * sl_
        d_pre = di[levels * c:(levels + 1) * c]
        d_suf = di[(levels + 1) * c:(levels + 2) * c]
        o_intra.append(jnp.dot(scores.astype(BF16), vi, preferred_element_type=F32))
        q_in.append((qi * d_pre).astype(BF16))
        upds.append(lax.dot_general(vi, (ki * d_suf).astype(BF16), (((0,), (0,)), ((), ())),
                                    preferred_element_type=F32))
        d_last.append(d_pre[c - 1:c, :])
    st = st_ref[...]
    outs = []
    for i in range(n_chunks):
        outs.append(o_intra[i] + lax.dot_general(q_in[i], st.astype(BF16), (((1,), (1,)), ((), ())),
                                                 preferred_element_type=F32))
        st = st * d_last[i] + upds[i]
    st_ref[...] = st
    o = jnp.concatenate(outs, axis=0)
    gt = g_ref[...].astype(F32)
    o_ref[...] = (_rms(o, gn_ref[...]) * (gt * _sigmoid(gt))).astype(o_ref.dtype)


def _hgrn2_core(qvg, z, lb, g_norm, batch, seq):
    t = z.shape[0]
    heads = z.shape[1] // LANES
    ts = _tile(seq, HG_TILE)
    ns = seq // ts
    c = min(HG_CHUNK, ts)
    tab, msk, levels = _hg_tables(c)
    tab = jnp.asarray(tab, BF16)
    msk = jnp.asarray(msk, F32)
    return pl.pallas_call(
        functools.partial(_hgrn2_kernel, chunk=c, levels=levels),
        grid=(batch, heads, ns),
        in_specs=[
            pl.BlockSpec((ts, LANES), lambda b, h, s: (b * ns + s, h)),
            pl.BlockSpec((ts, LANES), lambda b, h, s: (b * ns + s, h)),
            pl.BlockSpec((ts, LANES), lambda b, h, s: (b * ns + s, heads + h)),
            pl.BlockSpec((ts, LANES), lambda b, h, s: (b * ns + s, 2 * heads + h)),
            pl.BlockSpec((1, LANES), lambda b, h, s: (0, h)),
            pl.BlockSpec((1, LANES), lambda b, h, s: (0, 0)),
            pl.BlockSpec(tab.shape, lambda b, h, s: (0, 0)),
            pl.BlockSpec(msk.shape, lambda b, h, s: (0, 0)),
        ],
        out_specs=pl.BlockSpec((ts, LANES), lambda b, h, s: (b * ns + s, h)),
        out_shape=jax.ShapeDtypeStruct((t, heads * LANES), BF16),
        scratch_shapes=[pltpu.VMEM((LANES, LANES), F32)],
        compiler_params=_params("parallel", "parallel", "arbitrary"),
        name="hgrn2_recurrence",
    )(qvg, z, qvg, qvg, lb.reshape(1, -1).astype(F32), g_norm.reshape(1, -1).astype(F32), tab, msk)


def _hgrn2(x, ln, w_in, lb, g_norm, w_o, batch, seq):
    d = x.shape[1]
    assert d == HG_HEADS * LANES
    w_qvg = jnp.concatenate([w_in[:, :d], w_in[:, 2 * d:]], axis=1).astype(BF16)
    w_z = w_in[:, d:2 * d].astype(BF16)
    qvg = _pro_matmul(x, ln, w_qvg, prologue="rms", out_dtype=BF16, name="hg_in_proj_qvg")
    z = _pro_matmul(x, ln, w_z, prologue="rms", out_dtype=F32, name="hg_in_proj_z")
    o = _hgrn2_core(qvg, z, lb, g_norm, batch, seq)
    return _matmul_res(o, w_o.astype(BF16), x, name="hg_out_proj")


def _s5_tables(a_re, a_im, b_re, b_im, c_re, c_im, d_skip, log_dt, nblk):
    L = S5_BLOCK
    G, P = a_re.shape
    N = b_re.shape[-1]
    a = lax.complex(a_re.astype(F32), a_im.astype(F32))
    dt_a = a * jnp.exp(log_dt.astype(F32))[:, None]
    a_bar = jnp.exp(dt_a)
    b_bar = ((a_bar - 1.0) / a)[:, :, None] * lax.complex(b_re.astype(F32), b_im.astype(F32))
    cc = lax.complex(c_re.astype(F32), c_im.astype(F32))
    pw = jnp.exp(jnp.arange(L + 1, dtype=F32)[:, None, None] * dt_a)
    kern = jnp.real(jnp.einsum("gnp,tgp,gpm->gtnm", cc, pw[:L], b_bar))
    kern = kern.at[:, 0].add(jax.vmap(jnp.diag)(d_skip.astype(F32)))
    s_idx = jnp.arange(L)[:, None]
    t_idx = jnp.arange(L)[None, :]
    lag = t_idx - s_idx
    toep = jnp.where((lag >= 0)[None, :, :, None, None], kern[:, jnp.clip(lag, 0, L - 1)], 0.0)
    toep = toep.transpose(0, 1, 4, 2, 3).reshape(G, L * N, L * N)
    wst = jnp.einsum("sgp,gpm->gsmp", pw[:L][::-1], b_bar)
    w_st = jnp.concatenate([jnp.real(wst), jnp.imag(wst)], axis=-1).reshape(G, L * N, 2 * P)
    co = jnp.einsum("gnp,tgp->gptn", cc, pw[1:])
    c_out = jnp.concatenate([jnp.real(co), -jnp.imag(co)], axis=1).reshape(G, 2 * P, L * N)
    nsteps = max(1, int(math.ceil(math.log2(nblk))))
    lam_k = jnp.exp((L * 2.0 ** jnp.arange(nsteps, dtype=F32))[None, :, None] * dt_a[:, None, :])
    lr, li = jnp.real(lam_k), jnp.imag(lam_k)
    lam = jnp.stack([jnp.concatenate([lr, lr], -1), jnp.concatenate([-li, li], -1)], axis=2)
    return toep.astype(BF16), w_st.astype(BF16), c_out.astype(BF16), lam.astype(F32)


def _s5_kernel(u_ref, toep_ref, wst_ref, cout_ref, lam_ref, y_ref, *, nsteps):
    L, N = S5_BLOCK, S5_GROUP
    nblk = u_ref.shape[0] // L
    gpl = LANES // N
    nchunk = L // gpl
    lane_grp = lax.broadcasted_iota(jnp.int32, (1, LANES), 1) // N

    def place(pieces, shifts):
        out = None
        for i, (p, s) in enumerate(zip(pieces, shifts)):
            r = p if s % gpl == 0 else pltpu.roll(p, (N * s) % LANES, axis=1)
            out = r if out is None else jnp.where(lane_grp == i, r, out)
        return out

    v = [u_ref[pl.ds(t, nblk, stride=L), :] for t in range(L)]
    blk = lax.broadcasted_iota(jnp.int32, (nblk, wst_ref.shape[-1]), 0)
    ys = []
    for g in range(gpl):
        ug = jnp.concatenate(
            [place([v[j * gpl + tau] for tau in range(gpl)], [tau - g for tau in range(gpl)])
             for j in range(nchunk)], axis=1).astype(BF16)
        z = jnp.dot(ug, wst_ref[g], preferred_element_type=F32)
        p2 = z.shape[1]
        h = z
        for k in range(nsteps):
            sh = 1 << k
            prev = jnp.where(blk >= sh, pltpu.roll(h, sh, axis=0), 0.0)
            lam = lam_ref[g, k]
            h = h + prev * lam[0:1, :] + pltpu.roll(prev, p2 // 2, axis=1) * lam[1:2, :]
        h0 = jnp.where(blk >= 1, pltpu.roll(h, 1, axis=0), 0.0)
        ys.append(jnp.dot(ug, toep_ref[g], preferred_element_type=F32)
                  + jnp.dot(h0.astype(BF16), cout_ref[g], preferred_element_type=F32))
    for t in range(L):
        j, tau = divmod(t, gpl)
        w = place([ys[g][:, j * LANES:(j + 1) * LANES] for g in range(gpl)], [g - tau for g in range(gpl)])
        y_ref[pl.ds(t, nblk, stride=L), :] = w


def _s5_core(u, toep, w_st, c_out, lam, batch, seq):
    t, d = u.shape
    gpl = LANES // S5_GROUP
    ln = toep.shape[-1]
    p2 = w_st.shape[-1]
    nsteps = lam.shape[1]
    return pl.pallas_call(
        functools.partial(_s5_kernel, nsteps=nsteps),
        grid=(d // LANES, batch),
        in_specs=[
            pl.BlockSpec((seq, LANES), lambda i, b: (b, i)),
            pl.BlockSpec((gpl, ln, ln), lambda i, b: (i, 0, 0)),
            pl.BlockSpec((gpl, ln, p2), lambda i, b: (i, 0, 0)),
            pl.BlockSpec((gpl, p2, ln), lambda i, b: (i, 0, 0)),
            pl.BlockSpec((gpl, nsteps, 2, p2), lambda i, b: (i, 0, 0, 0)),
        ],
        out_specs=pl.BlockSpec((seq, LANES), lambda i, b: (b, i)),
        out_shape=jax.ShapeDtypeStruct((t, d), F32),
        compiler_params=_params("parallel", "parallel"),
        name="s5_ssm",
    )(u, toep, w_st, c_out, lam)


def _s5(x, ln, w_in, a_re, a_im, b_re, b_im, c_re, c_im, d_skip, log_dt, w_out, batch, seq):
    nblk = seq // S5_BLOCK
    u = _pro_matmul(x, ln, w_in.astype(BF16), prologue="rms", out_dtype=F32, name="s5_in_proj")
    toep, w_st, c_out, lam = _s5_tables(a_re, a_im, b_re, b_im, c_re, c_im, d_skip, log_dt, nblk)
    y = _s5_core(u, toep, w_st, c_out, lam, batch, seq)
    return _s5_out(y, w_out.astype(BF16), x)


def _router_kernel(x_ref, g_ref, w_ref, h_ref, r_ref):
    h = _rms(x_ref[...], g_ref[...])
    h_ref[...] = h
    h0 = h.astype(BF16)
    h1 = (h - h0.astype(F32)).astype(BF16)
    w = w_ref[...]
    w0 = w.astype(BF16)
    w1 = (w - w0.astype(F32)).astype(BF16)
    logits = (jnp.dot(h0, w0, preferred_element_type=F32) + jnp.dot(h0, w1, preferred_element_type=F32)
              + jnp.dot(h1, w0, preferred_element_type=F32))
    lane = lax.broadcasted_iota(jnp.int32, logits.shape, 1).astype(F32)
    neg = -jnp.inf
    lg = jnp.where(lane < N_EXPERTS, logits, neg)
    m1 = jnp.max(lg, axis=-1, keepdims=True)
    i1 = jnp.min(jnp.where(lg == m1, lane, float(LANES)), axis=-1, keepdims=True)
    lg2 = jnp.where(lane == i1, neg, lg)
    m2 = jnp.max(lg2, axis=-1, keepdims=True)
    i2 = jnp.min(jnp.where(lg2 == m2, lane, float(LANES)), axis=-1, keepdims=True)
    e2 = jnp.exp(m2 - m1)
    g1 = 1.0 / (1.0 + e2)
    g2 = e2 / (1.0 + e2)
    out = jnp.where(lane == 0, g1, jnp.where(lane == 1, g2, jnp.where(lane == 2, i1, jnp.where(lane == 3, i2, 0.0))))
    r_ref[...] = out[:, :r_ref.shape[1]]


def _router(x, gain, w_router):
    m, d = x.shape
    tm = _tile(m, TM)
    w_pad = jnp.zeros((d, LANES), F32).at[:, :N_EXPERTS].set(w_router.astype(F32))
    return pl.pallas_call(
        _router_kernel,
        grid=(m // tm,),
        in_specs=[
            pl.BlockSpec((tm, d), lambda i: (i, 0)),
            pl.BlockSpec((1, d), lambda i: (0, 0)),
            pl.BlockSpec((d, LANES), lambda i: (0, 0)),
        ],
        out_specs=[pl.BlockSpec((tm, d), lambda i: (i, 0)), pl.BlockSpec((tm, 8), lambda i: (i, 0))],
        out_shape=[jax.ShapeDtypeStruct((m, d), F32), jax.ShapeDtypeStruct((m, 8), F32)],
        compiler_params=_params("parallel"),
        name="moe_router_top2",
    )(x, gain.reshape(1, d).astype(F32), w_pad)


def _moe_kernel(te_ref, ta_ref, tv_ref, idx_ref, h_hbm, wg_ref, wu_ref, wd_ref, y_hbm,
                x_ref, xb_ref, acc_ref, yb_ref, pend_ref, sem_g, sem_s):
    i = pl.program_id(0)
    f = pl.program_id(1)
    nt = pl.num_programs(0)
    last_f = pl.num_programs(1) - 1
    tm = xb_ref.shape[0]
    slot = lax.rem(i, 2)
    active = ta_ref[i] == 1
    nxt = jnp.minimum(i + 1, nt - 1)
    carry = active & (i + 1 < nt) & (ta_ref[nxt] == 1)
    n_dma_steps = tm // GATHER_CHUNK

    def start_gather(tile, sl):
        def body(o, c):
            base = pl.multiple_of(o * DMA_UNROLL, DMA_UNROLL)
            for j in range(DMA_UNROLL):
                r = base + j
                pltpu.make_async_copy(h_hbm.at[pl.ds(idx_ref[tile, r], 1)], x_ref.at[sl, pl.ds(r, 1)],
                                      sem_g.at[sl]).start()
            return c

        lax.fori_loop(0, tm // DMA_UNROLL, body, 0)

    def wait_gather(sl):
        pltpu.make_async_copy(h_hbm.at[pl.ds(0, tm)], x_ref.at[sl], sem_g.at[sl]).wait()

    def scatter_row(r):
        pltpu.make_async_copy(yb_ref.at[pl.ds(r, 1)], y_hbm.at[pl.ds(idx_ref[i, tm + r], 1)], sem_s).start()

    def start_scatter():
        @pl.when(tv_ref[i] == tm)
        def _():
            def body(o, c):
                base = pl.multiple_of(o * DMA_UNROLL, DMA_UNROLL)
                for j in range(DMA_UNROLL):
                    scatter_row(base + j)
                return c

            lax.fori_loop(0, tm // DMA_UNROLL, body, 0)

        @pl.when(tv_ref[i] < tm)
        def _():
            def body(r, c):
                scatter_row(r)
                return c

            lax.fori_loop(0, tv_ref[i], body, 0)

    def wait_scatter():
        @pl.when(pend_ref[0] > 0)
        def _():
            nv = tv_ref[jnp.maximum(pend_ref[0] - 1, 0)]
            for bit in reversed(range(tm.bit_length())):
                n = 1 << bit

                @pl.when((nv & n) != 0)
                def _():
                    pltpu.make_async_copy(yb_ref.at[pl.ds(0, n)], y_hbm.at[pl.ds(0, n)], sem_s).wait()

            pend_ref[0] = 0

    @pl.when((i == 0) & (f == 0))
    def _():
        pend_ref[0] = 0

        @pl.when(active)
        def _():
            start_gather(0, 0)

    @pl.when(active & (f == 0))
    def _():
        wait_gather(slot)
        xb_ref[...] = x_ref[slot].astype(BF16)
        acc_ref[...] = jnp.zeros_like(acc_ref)

    @pl.when(carry & (f < n_dma_steps))
    def _():
        base = pl.multiple_of(f * GATHER_CHUNK, GATHER_CHUNK)
        for j in range(GATHER_CHUNK):
            r = base + j
            pltpu.make_async_copy(h_hbm.at[pl.ds(idx_ref[nxt, r], 1)], x_ref.at[1 - slot, pl.ds(r, 1)],
                                  sem_g.at[1 - slot]).start()
        acc_ref[...] += _swiglu_step(xb_ref[...], wg_ref[...], wu_ref[...], wd_ref[...])

    @pl.when(active & jnp.logical_not(carry & (f < n_dma_steps)))
    def _():
        acc_ref[...] += _swiglu_step(xb_ref[...], wg_ref[...], wu_ref[...], wd_ref[...])

    @pl.when(active & (f == last_f))
    def _():
        wait_scatter()
        yb_ref[...] = acc_ref[...]
        start_scatter()
        pend_ref[0] = i + 1

    @pl.when((i == nt - 1) & (f == last_f))
    def _():
        wait_scatter()


def _moe_experts(h, idx, tile_expert, tile_active, tile_valid, w_gu, w_down, layer):
    n_tok, d = h.shape
    n_tiles, tm = idx.shape[0], idx.shape[1] // 2
    ff = w_down.shape[2]
    tf = _tile(ff, TF, LANES)
    nf = ff // tf
    assert tm % GATHER_CHUNK == 0 and tm // GATHER_CHUNK <= nf and tm % DMA_UNROLL == 0

    def fe(f, i, ta):
        return jnp.where(ta[i] == 1, f, nf - 1)

    grid_spec = pltpu.PrefetchScalarGridSpec(
        num_scalar_prefetch=4,
        grid=(n_tiles, nf),
        in_specs=[
            pl.BlockSpec(memory_space=pl.ANY),
            pl.BlockSpec((None, None, d, tf), lambda i, f, te, ta, tv, ix: (layer, te[i], 0, fe(f, i, ta))),
            pl.BlockSpec((None, None, d, tf), lambda i, f, te, ta, tv, ix: (layer, te[i], 0, fe(f, i, ta) + nf)),
            pl.BlockSpec((None, None, tf, d), lambda i, f, te, ta, tv, ix: (layer, te[i], fe(f, i, ta), 0)),
        ],
        out_specs=pl.BlockSpec(memory_space=pl.ANY),
        scratch_shapes=[
            pltpu.VMEM((2, tm, d), F32),
            pltpu.VMEM((tm, d), BF16),
            pltpu.VMEM((tm, d), F32),
            pltpu.VMEM((tm, d), F32),
            pltpu.SMEM((1,), jnp.int32),
            pltpu.SemaphoreType.DMA((2,)),
            pltpu.SemaphoreType.DMA,
        ],
    )
    return pl.pallas_call(
        _moe_kernel,
        grid_spec=grid_spec,
        out_shape=jax.ShapeDtypeStruct((TOP_K * n_tok, d), F32),
        compiler_params=_params("arbitrary", "arbitrary"),
        name="moe_grouped_swiglu",
    )(tile_expert, tile_active, tile_valid, idx, h, w_gu, w_gu, w_down)


def _combine_kernel(x_ref, a_ref, b_ref, r_ref, o_ref):
    r = r_ref[...]
    o_ref[...] = x_ref[...] + (r[:, 0:1] * a_ref[...] + r[:, 1:2] * b_ref[...])


def _combine(x, y, route):
    m, d = x.shape
    tm = _tile(m, TM)
    nb = m // tm
    return pl.pallas_call(
        _combine_kernel,
        grid=(nb,),
        in_specs=[
            pl.BlockSpec((tm, d), lambda i: (i, 0)),
            pl.BlockSpec((tm, d), lambda i: (i, 0)),
            pl.BlockSpec((tm, d), lambda i: (i + nb, 0)),
            pl.BlockSpec((tm, route.shape[1]), lambda i: (i, 0)),
        ],
        out_specs=pl.BlockSpec((tm, d), lambda i: (i, 0)),
        out_shape=jax.ShapeDtypeStruct((m, d), F32),
        compiler_params=_params("parallel"),
        name="moe_combine",
    )(x, y, y, route)


def _route_metadata(route, n_tok, tm):
    experts = route[:, TOP_K:2 * TOP_K].T.reshape(-1).astype(jnp.int32)
    n_pairs = TOP_K * n_tok
    onehot = (experts[:, None] == jnp.arange(N_EXPERTS)[None, :]).astype(jnp.int32)
    rank = jnp.take_along_axis(jnp.cumsum(onehot, axis=0), experts[:, None], axis=1)[:, 0] - 1
    counts = jnp.sum(onehot, axis=0)
    padded = ((counts + tm - 1) // tm) * tm
    ends = jnp.cumsum(padded)
    starts = ends - padded
    rows = starts[experts] + rank
    n_rows = n_pairs + N_EXPERTS * tm
    n_rows = (n_rows // tm) * tm
    dest = jnp.full((n_rows,), -1, jnp.int32).at[rows].set(jnp.arange(n_pairs, dtype=jnp.int32))
    src = jnp.where(dest < 0, 0, dest % n_tok)
    idx = jnp.concatenate([src.reshape(-1, tm), jnp.maximum(dest, 0).reshape(-1, tm)], axis=1)
    tile_start = jnp.arange(n_rows // tm, dtype=jnp.int32) * tm
    tile_expert = jnp.sum((tile_start[:, None] >= ends[None, :]).astype(jnp.int32), axis=1)
    tile_expert = jnp.minimum(tile_expert, N_EXPERTS - 1)
    tile_active = (tile_start < ends[-1]).astype(jnp.int32)
    tile_valid = jnp.clip((starts + counts)[tile_expert] - tile_start, 0, tm).astype(jnp.int32) * tile_active
    return idx, tile_expert, tile_active, tile_valid


def _moe(x, gain, w_router, w_gu, w_down, layer):
    n_tok = x.shape[0]
    h, route = _router(x, gain, w_router)
    tm = min(MOE_TM, n_tok)
    idx, tile_expert, tile_active, tile_valid = _route_metadata(route, n_tok, tm)
    y = _moe_experts(h, idx, tile_expert, tile_active, tile_valid, w_gu, w_down, layer)
    return _combine(x, y, route)


def _final_norm_kernel(x_ref, g_ref, o_ref):
    o_ref[...] = _rms(x_ref[...], g_ref[...])


def _final_norm(x, gain):
    m, d = x.shape
    tm = _tile(m, TM)
    return pl.pallas_call(
        _final_norm_kernel,
        grid=(m // tm,),
        in_specs=[pl.BlockSpec((tm, d), lambda i: (i, 0)), pl.BlockSpec((1, d), lambda i: (0, 0))],
        out_specs=pl.BlockSpec((tm, d), lambda i: (i, 0)),
        out_shape=jax.ShapeDtypeStruct((m, d), F32),
        compiler_params=_params("parallel"),
        name="final_rmsnorm",
    )(x, gain.reshape(1, d).astype(F32))


def kernel(x, ln_mix, ln_ffn, ln_final, mla_w_in, mla_q_norm, mla_kv_norm, mla_w_uq, mla_w_ukv, mla_w_o, hg_w_in, hg_lower_bound, hg_g_norm, hg_w_o, s5_w_in, s5_a_re, s5_a_im, s5_b_re, s5_b_im, s5_c_re, s5_c_im, s5_d, s5_log_dt, s5_w_out, ffn_w_gu, ffn_w_down, moe_w_router, moe_w_gu, moe_w_down):
    batch, seq, d = x.shape
    depth = ln_mix.shape[0]
    lb_w = jax.nn.softmax(hg_lower_bound.astype(F32), axis=0)
    lower_bounds = jnp.cumsum(lb_w, axis=0) - lb_w[0]
    xt = x.reshape(batch * seq, d).astype(F32)
    ffn_gu, ffn_down = ffn_w_gu.astype(BF16), ffn_w_down.astype(BF16)
    moe_gu, moe_down = moe_w_gu.astype(BF16), moe_w_down.astype(BF16)
    for i in range(depth):
        m, j = i % N_MIXERS, i // N_MIXERS
        if m == 0:
            xt = _mla(xt, ln_mix[i], mla_w_in[j], mla_q_norm[j], mla_kv_norm[j], mla_w_uq[j], mla_w_ukv[j],
                      mla_w_o[j], batch, seq)
        elif m == 1:
            xt = _hgrn2(xt, ln_mix[i], hg_w_in[j], lower_bounds[i], hg_g_norm[j], hg_w_o[j], batch, seq)
        else:
            xt = _s5(xt, ln_mix[i], s5_w_in[j], s5_a_re[j], s5_a_im[j], s5_b_re[j], s5_b_im[j], s5_c_re[j],
                     s5_c_im[j], s5_d[j], s5_log_dt[j], s5_w_out[j], batch, seq)
        f = i // 2
        if i % 2 == 0:
            xt = _ffn(xt, ln_ffn[i], ffn_gu, ffn_down, f)
        else:
            xt = _moe(xt, ln_ffn[i], moe_w_router[f], moe_gu, moe_down, f)
    return _final_norm(xt, ln_final).reshape(batch, seq, d)
```

```python
import functools
import math

import numpy as np
import jax
import jax.numpy as jnp
from jax import lax
from jax.experimental import pallas as pl
from jax.experimental.pallas import tpu as pltpu

F32 = jnp.float32
BF16 = jnp.bfloat16
EPS = 1e-6
N_MIXERS = 3

MLA_HEADS = 16
MLA_Q_RANK = 512
MLA_KV_RANK = 512
MLA_NOPE = 128
MLA_ROPE = 64
MLA_V = 128
ROPE_THETA = 10000.0
HG_HEADS = 16
HG_CHUNK = 64
S5_GROUP = 16
S5_STATE = 64
S5_BLOCK = 32
N_EXPERTS = 8
TOP_K = 2

LANES = 128
VMEM_LIMIT = 56 * 1024 * 1024
TILE_VMEM_BUDGET = 40 * 1024 * 1024
RESIDENT_WEIGHT_BYTES = 8 * 1024 * 1024

TM = 512
TN = 1024
TF = 512
ATT_TILE = 1024
ATT_GROUPS = 4
ATT_HEADS = 2
HG_TILE = 512
MOE_TM = 512
DMA_UNROLL = 8
GATHER_CHUNK = 64


def _tile(n, t, step=8):
    if n % step:
        return n
    best = step
    for c in range(step, min(n, t) + 1, step):
        if n % c == 0:
            best = c
    return best


def _matmul_tiles(m, k, n, vmem_bytes):
    tn = n if 2 * k * n <= RESIDENT_WEIGHT_BYTES else _tile(n, TN, LANES)
    for tm in (_tile(m, 2 * TM), _tile(m, TM)):
        if vmem_bytes(tm, tn) <= TILE_VMEM_BUDGET:
            break
    return tm, tn


def _params(*sem):
    return pltpu.CompilerParams(dimension_semantics=sem, vmem_limit_bytes=VMEM_LIMIT)


def _rms(x, g):
    return x * lax.rsqrt(jnp.mean(x * x, axis=-1, keepdims=True) + EPS) * g


def _gelu_tanh(y):
    c = math.sqrt(2.0 / math.pi)
    return 0.5 * y * (1.0 + jnp.tanh(c * (y + 0.044715 * (y * y * y))))


def _sigmoid(x):
    return 1.0 / (1.0 + jnp.exp(-x))


def _pro_matmul_kernel(x_ref, g_ref, w_ref, o_ref, h_ref, *, prologue):
    @pl.when(pl.program_id(1) == 0)
    def _():
        x = x_ref[...].astype(F32)
        if prologue == "rms":
            x = _rms(x, g_ref[...])
        h_ref[...] = x.astype(BF16)

    o_ref[...] = jnp.dot(h_ref[...], w_ref[...], preferred_element_type=F32).astype(o_ref.dtype)


def _pro_matmul(x, gain, w, *, prologue, out_dtype, x_col_block=0, name):
    m = x.shape[0]
    k, n = w.shape
    xb, ob = x.dtype.itemsize, jnp.dtype(out_dtype).itemsize
    tm, tn = _matmul_tiles(m, k, n, lambda tm, tn: 2 * tm * k * xb + 2 * tm * k + 4 * k * tn + 2 * tm * tn * ob)
    return pl.pallas_call(
        functools.partial(_pro_matmul_kernel, prologue=prologue),
        grid=(m // tm, n // tn),
        in_specs=[
            pl.BlockSpec((tm, k), lambda i, j: (i, x_col_block)),
            pl.BlockSpec((1, k), lambda i, j: (0, 0)),
            pl.BlockSpec((k, tn), lambda i, j: (0, j)),
        ],
        out_specs=pl.BlockSpec((tm, tn), lambda i, j: (i, j)),
        out_shape=jax.ShapeDtypeStruct((m, n), out_dtype),
        scratch_shapes=[pltpu.VMEM((tm, k), BF16)],
        compiler_params=_params("parallel", "arbitrary"),
        name=name,
    )(x, gain.reshape(1, k).astype(F32), w)


def _matmul_res_kernel(a_ref, w_ref, r_ref, o_ref):
    o_ref[...] = r_ref[...] + jnp.dot(a_ref[...], w_ref[...], preferred_element_type=F32)


def _matmul_res(a, w, res, *, name):
    m, k = a.shape
    n = w.shape[1]
    tm, tn = _matmul_tiles(m, k, n, lambda tm, tn: 4 * tm * k + 4 * k * tn + 16 * tm * tn)
    return pl.pallas_call(
        _matmul_res_kernel,
        grid=(m // tm, n // tn),
        in_specs=[
            pl.BlockSpec((tm, k), lambda i, j: (i, 0)),
            pl.BlockSpec((k, tn), lambda i, j: (0, j)),
            pl.BlockSpec((tm, tn), lambda i, j: (i, j)),
        ],
        out_specs=pl.BlockSpec((tm, tn), lambda i, j: (i, j)),
        out_shape=jax.ShapeDtypeStruct((m, n), F32),
        compiler_params=_params("parallel", "parallel"),
        name=name,
    )(a, w, res)


def _s5_out_kernel(y_ref, wv_ref, wg_ref, r_ref, o_ref, h_ref):
    @pl.when(pl.program_id(1) == 0)
    def _():
        h_ref[...] = _gelu_tanh(y_ref[...].astype(F32)).astype(BF16)

    h = h_ref[...]
    val = jnp.dot(h, wv_ref[...], preferred_element_type=F32)
    gate = jnp.dot(h, wg_ref[...], preferred_element_type=F32)
    o_ref[...] = r_ref[...] + val * _sigmoid(gate)


def _s5_out(y, w_out, res):
    m, k = y.shape
    n = w_out.shape[1] // 2
    tm, tn = _tile(m, TM), _tile(n, TN, LANES)
    nb = n // tn
    return pl.pallas_call(
        _s5_out_kernel,
        grid=(m // tm, nb),
        in_specs=[
            pl.BlockSpec((tm, k), lambda i, j: (i, 0)),
            pl.BlockSpec((k, tn), lambda i, j: (0, j)),
            pl.BlockSpec((k, tn), lambda i, j: (0, j + nb)),
            pl.BlockSpec((tm, tn), lambda i, j: (i, j)),
        ],
        out_specs=pl.BlockSpec((tm, tn), lambda i, j: (i, j)),
        out_shape=jax.ShapeDtypeStruct((m, n), F32),
        scratch_shapes=[pltpu.VMEM((tm, k), BF16)],
        compiler_params=_params("parallel", "arbitrary"),
        name="s5_out_glu",
    )(y, w_out, w_out, res)


def _swiglu_step(h, wg, wu, wd):
    g = jnp.dot(h, wg, preferred_element_type=F32)
    u = jnp.dot(h, wu, preferred_element_type=F32)
    a = (g * _sigmoid(g) * u).astype(BF16)
    return jnp.dot(a, wd, preferred_element_type=F32)


def _ffn_kernel(x_ref, g_ref, wg_ref, wu_ref, wd_ref, o_ref, h_ref, acc_ref):
    f = pl.program_id(1)

    @pl.when(f == 0)
    def _():
        h_ref[...] = _rms(x_ref[...], g_ref[...]).astype(BF16)
        acc_ref[...] = jnp.zeros_like(acc_ref)

    acc_ref[...] += _swiglu_step(h_ref[...], wg_ref[...], wu_ref[...], wd_ref[...])

    @pl.when(f == pl.num_programs(1) - 1)
    def _():
        o_ref[...] = x_ref[...] + acc_ref[...]


def _ffn(x, gain, w_gu, w_down, layer):
    m, d = x.shape
    ff = w_down.shape[1]
    tm, tf = _tile(m, TM), _tile(ff, TF, LANES)
    nf = ff // tf
    return pl.pallas_call(
        _ffn_kernel,
        grid=(m // tm, nf),
        in_specs=[
            pl.BlockSpec((tm, d), lambda i, f: (i, 0)),
            pl.BlockSpec((1, d), lambda i, f: (0, 0)),
            pl.BlockSpec((None, d, tf), lambda i, f: (layer, 0, f)),
            pl.BlockSpec((None, d, tf), lambda i, f: (layer, 0, f + nf)),
            pl.BlockSpec((None, tf, d), lambda i, f: (layer, f, 0)),
        ],
        out_specs=pl.BlockSpec((tm, d), lambda i, f: (i, 0)),
        out_shape=jax.ShapeDtypeStruct((m, d), F32),
        scratch_shapes=[pltpu.VMEM((tm, d), BF16), pltpu.VMEM((tm, d), F32)],
        compiler_params=_params("parallel", "arbitrary"),
        name="ffn_swiglu",
    )(x, gain.reshape(1, d).astype(F32), w_gu, w_gu, w_down)


def _rope_table(seq):
    half = MLA_ROPE // 2
    inv = ROPE_THETA ** (-jnp.arange(half, dtype=F32) / half)
    ang = jnp.arange(seq, dtype=F32)[:, None] * inv
    cos, sin = jnp.cos(ang), jnp.sin(ang)
    return jnp.concatenate([cos, cos, sin, sin], axis=-1)


def _rot_cols(w):
    half = w.shape[-1] // 2
    return jnp.concatenate([-w[..., half:], w[..., :half]], axis=-1)


def _rope_slab(slab, cs):
    prod = slab * cs
    return prod + pltpu.roll(prod, MLA_ROPE, axis=1)


def _krope_kernel(p_ref, cs_ref, o_ref):
    r = _rope_slab(p_ref[...].astype(F32), cs_ref[...])
    lane = lax.broadcasted_iota(jnp.int32, r.shape, 1)
    o_ref[...] = jnp.where(lane < MLA_ROPE, r, 0.0).astype(o_ref.dtype)


def _krope(proj, cs, seq, col_block):
    t = proj.shape[0]
    tm = _tile(seq, TM)
    ns = seq // tm
    return pl.pallas_call(
        _krope_kernel,
        grid=(t // tm,),
        in_specs=[
            pl.BlockSpec((tm, LANES), lambda i: (i, col_block)),
            pl.BlockSpec((tm, LANES), lambda i: (i % ns, 0)),
        ],
        out_specs=pl.BlockSpec((tm, LANES), lambda i: (i, 0)),
        out_shape=jax.ShapeDtypeStruct((t, LANES), BF16),
        compiler_params=_params("parallel"),
        name="mla_k_rope",
    )(proj, cs)


def _attn_kernel(qi_ref, ki_ref, q_ref, cs_ref, kv_ref, kr_ref, o_ref, qc_ref, m_ref, acc_ref):
    s_id = pl.program_id(2)
    qi = qi_ref[s_id]
    ki = ki_ref[s_id]
    hw = 2 * LANES

    @pl.when(ki == 0)
    def _():
        for hh in range(ATT_HEADS):
            q = q_ref[:, hh * hw:(hh + 1) * hw]
            qr = _rope_slab(q[:, LANES:].astype(F32), cs_ref[...]).astype(BF16)
            qc_ref[hh] = jnp.concatenate([q[:, :LANES], qr], axis=1)
        m_ref[...] = jnp.full_like(m_ref, -jnp.inf)
        acc_ref[...] = jnp.zeros_like(acc_ref)

    lane = lax.broadcasted_iota(jnp.int32, kr_ref.shape, 1)
    ones_col = jnp.where(lane == 0, 1.0, 0.0).astype(BF16)
    rq = q_ref.shape[0] // ATT_GROUPS

    def sweep(masked):
        for hh in range(ATT_HEADS):
            kc = jnp.concatenate([kv_ref[:, hh * hw:hh * hw + LANES], kr_ref[...]], axis=1)
            ve = jnp.concatenate([kv_ref[:, hh * hw + LANES:(hh + 1) * hw], ones_col], axis=1)

            def scores(g):
                return lax.dot_general(qc_ref[hh, g * rq:(g + 1) * rq, :], kc, (((1,), (1,)), ((), ())),
                                       preferred_element_type=F32)

            def update(g, sc):
                rows = slice(g * rq, (g + 1) * rq)
                if masked:
                    row = lax.broadcasted_iota(jnp.int32, sc.shape, 0) + g * rq
                    col = lax.broadcasted_iota(jnp.int32, sc.shape, 1)
                    sc = jnp.where(col <= row, sc, -jnp.inf)
                m_prev = m_ref[hh, rows, :]
                m_new = jnp.maximum(m_prev, jnp.max(sc, axis=-1, keepdims=True))
                alpha = jnp.exp2(m_prev - m_new)
                p = jnp.exp2(sc - m_new)
                acc_ref[hh, rows, :] = alpha * acc_ref[hh, rows, :] + jnp.dot(p.astype(BF16), ve,
                                                                              preferred_element_type=F32)
                m_ref[hh, rows, :] = m_new

            nxt = scores(0)
            for g in range(ATT_GROUPS):
                cur = nxt
                if g + 1 < ATT_GROUPS:
                    nxt = scores(g + 1)
                update(g, cur)

    @pl.when(ki < qi)
    def _():
        sweep(False)

    @pl.when(ki == qi)
    def _():
        sweep(True)
        for hh in range(ATT_HEADS):
            a = acc_ref[hh]
            o_ref[:, hh * LANES:(hh + 1) * LANES] = (a[:, :LANES] / a[:, LANES:LANES + 1]).astype(o_ref.dtype)


def _attention(q_ext, kv_ext, k_rope, cs, batch, seq):
    t = q_ext.shape[0]
    heads = q_ext.shape[1] // (2 * LANES)
    assert heads % ATT_HEADS == 0
    hw = ATT_HEADS * 2 * LANES
    tq = _tile(seq, ATT_TILE)
    nq = seq // tq
    steps = [(i, j) for i in range(nq) for j in range(i + 1)]
    qi_tab = jnp.asarray([s[0] for s in steps], jnp.int32)
    ki_tab = jnp.asarray([s[1] for s in steps], jnp.int32)
    grid_spec = pltpu.PrefetchScalarGridSpec(
        num_scalar_prefetch=2,
        grid=(batch, heads // ATT_HEADS, len(steps)),
        in_specs=[
            pl.BlockSpec((tq, hw), lambda b, h, s, qi, ki: (b * nq + qi[s], h)),
            pl.BlockSpec((tq, LANES), lambda b, h, s, qi, ki: (qi[s], 0)),
            pl.BlockSpec((tq, hw), lambda b, h, s, qi, ki: (b * nq + ki[s], h)),
            pl.BlockSpec((tq, LANES), lambda b, h, s, qi, ki: (b * nq + ki[s], 0)),
        ],
        out_specs=pl.BlockSpec((tq, ATT_HEADS * LANES), lambda b, h, s, qi, ki: (b * nq + qi[s], h)),
        scratch_shapes=[
            pltpu.VMEM((ATT_HEADS, tq, 2 * LANES), BF16),
            pltpu.VMEM((ATT_HEADS, tq, 1), F32),
            pltpu.VMEM((ATT_HEADS, tq, 2 * LANES), F32),
        ],
    )
    return pl.pallas_call(
        _attn_kernel,
        grid_spec=grid_spec,
        out_shape=jax.ShapeDtypeStruct((t, heads * LANES), BF16),
        compiler_params=_params("parallel", "parallel", "arbitrary"),
        name="mla_flash_attention",
    )(qi_tab, ki_tab, q_ext, cs, kv_ext, k_rope)


def _mla(x, ln, w_in, q_norm, kv_norm, w_uq, w_ukv, w_o, batch, seq):
    heads = MLA_HEADS
    assert MLA_NOPE == LANES and MLA_V == LANES and 2 * MLA_ROPE == LANES
    assert MLA_Q_RANK == MLA_KV_RANK
    rank = MLA_Q_RANK
    w_kr = w_in[:, 2 * rank:]
    w_in_ext = jnp.concatenate([w_in, _rot_cols(w_kr)], axis=1).astype(BF16)
    proj = _pro_matmul(x, ln, w_in_ext, prologue="rms", out_dtype=F32, name="mla_in_proj")
    scale = (MLA_NOPE + MLA_ROPE) ** -0.5 * math.log2(math.e)
    wq = w_uq.reshape(rank, heads, MLA_NOPE + MLA_ROPE) * scale
    wq_ext = jnp.concatenate([wq, _rot_cols(wq[..., MLA_NOPE:])], axis=-1)
    wq_ext = wq_ext.reshape(rank, heads * 2 * LANES).astype(BF16)
    q_ext = _pro_matmul(proj, q_norm, wq_ext, prologue="rms", out_dtype=BF16, x_col_block=0, name="mla_q_up")
    kv_ext = _pro_matmul(proj, kv_norm, w_ukv.astype(BF16), prologue="rms", out_dtype=BF16, x_col_block=1,
                         name="mla_kv_up")
    cs = _rope_table(seq)
    k_rope = _krope(proj, cs, seq, col_block=2 * rank // LANES)
    o = _attention(q_ext, kv_ext, k_rope, cs, batch, seq)
    return _matmul_res(o, w_o.astype(BF16), x, name="mla_out_proj")


def _hg_tables(c):
    levels = int(math.log2(c))
    assert 2 ** levels == c
    t = np.arange(c)[:, None]
    u = np.arange(c)[None, :]
    mats = []
    for l in range(1, levels + 1):
        mid = ((t >> l) << l) + (1 << (l - 1))
        upper = t >= mid
        mats.append(np.where(upper, (u >= mid) & (u <= t), (u > t) & (u <= mid - 1)))
    mats.append(u <= t)
    mats.append(u > t)
    masks = [t == u]
    for l in range(1, levels + 1):
        masks.append(((t >> l) == (u >> l)) & (((t >> (l - 1)) & 1) == 1) & (((u >> (l - 1)) & 1) == 0))
    return (np.concatenate(mats, axis=0).astype(np.float32), np.concatenate(masks, axis=0).astype(np.float32),
            levels)


def _split3(x):
    hi = x.astype(BF16)
    r = x - hi.astype(F32)
    mid = r.astype(BF16)
    lo = (r - mid.astype(F32)).astype(BF16)
    return hi, mid, lo


def _hgrn2_kernel(q_ref, z_ref, v_ref, g_ref, lb_ref, gn_ref, tab_ref, msk_ref, o_ref, st_ref, *, chunk, levels):
    c = chunk
    n_chunks = q_ref.shape[0] // c
    dk = q_ref.shape[1]

    @pl.when(pl.program_id(2) == 0)
    def _():
        st_ref[...] = jnp.zeros_like(st_ref)

    lb = lb_ref[...]
    log_lb = jnp.log(lb)
    log_1mlb = jnp.log1p(-lb)
    z = z_ref[...]
    q = q_ref[...].astype(F32)
    e = jnp.exp(-jnp.abs(z))
    log_sig = jnp.minimum(z, 0.0) - jnp.log1p(e)
    bterm = log_1mlb + log_sig
    lf = jnp.maximum(log_lb, bterm) + jnp.log1p(jnp.exp(-jnp.abs(log_lb - bterm)))
    k = (1.0 - lb) * jnp.where(z >= 0, e, 1.0) / (1.0 + e)

    def chunks_on_lanes(x):
        return jnp.concatenate([x[i * c:(i + 1) * c] for i in range(n_chunks)], axis=1)

    tab = tab_ref[...]
    p0, p1, p2 = _split3(lf)
    sums = (jnp.dot(tab, chunks_on_lanes(p0), preferred_element_type=F32)
            + jnp.dot(tab, chunks_on_lanes(p1), preferred_element_type=F32)
            + jnp.dot(tab, chunks_on_lanes(p2), preferred_element_type=F32))
    dec = jnp.exp(sums)
    diag = jnp.sum(q * k, axis=-1, keepdims=True)
    o_intra, q_in, upds, d_last = [], [], [], []
    for i in range(n_chunks):
        rows = slice(i * c, (i + 1) * c)
        qi, ki, vi = q[rows], k[rows], v_ref[rows, :]
        di = dec[:, i * dk:(i + 1) * dk]
        scores = msk_ref[0:c, :] * diag[rows]
        for l in range(1, levels + 1):
            d = di[(l - 1) * c:l * c]
            sl_ = lax.dot_general((qi * d).astype(BF16), (ki * d).astype(BF16), (((1,), (1,)), ((), ())),
                                  preferred_element_type=F32)
            scores = scores + msk_ref[l * c:(l + 1) * c, :] * sl_
        d_pre = di[levels * c:(levels + 1) * c]
        d_suf = di[(levels + 1) * c:(levels + 2) * c]
        o_intra.append(jnp.dot(scores.astype(BF16), vi, preferred_element_type=F32))
        q_in.append((qi * d_pre).astype(BF16))
        upds.append(lax.dot_general(vi, (ki * d_suf).astype(BF16), (((0,), (0,)), ((), ())),
                                    preferred_element_type=F32))
        d_last.append(d_pre[c - 1:c, :])
    st = st_ref[...]
    outs = []
    for i in range(n_chunks):
        outs.append(o_intra[i] + lax.dot_general(q_in[i], st.astype(BF16), (((1,), (1,)), ((), ())),
                                                 preferred_element_type=F32))
        st = st * d_last[i] + upds[i]
    st_ref[...] = st
    o = jnp.concatenate(outs, axis=0)
    gt = g_ref[...].astype(F32)
    o_ref[...] = (_rms(o, gn_ref[...]) * (gt * _sigmoid(gt))).astype(o_ref.dtype)


def _hgrn2_core(qvg, z, lb, g_norm, batch, seq):
    t = z.shape[0]
    heads = z.shape[1] // LANES
    ts = _tile(seq, HG_TILE)
    ns = seq // ts
    c = min(HG_CHUNK, ts)
    tab, msk, levels = _hg_tables(c)
    tab = jnp.asarray(tab, BF16)
    msk = jnp.asarray(msk, F32)
    return pl.pallas_call(
        functools.partial(_hgrn2_kernel, chunk=c, levels=levels),
        grid=(batch, heads, ns),
        in_specs=[
            pl.BlockSpec((ts, LANES), lambda b, h, s: (b * ns + s, h)),
            pl.BlockSpec((ts, LANES), lambda b, h, s: (b * ns + s, h)),
            pl.BlockSpec((ts, LANES), lambda b, h, s: (b * ns + s, heads + h)),
            pl.BlockSpec((ts, LANES), lambda b, h, s: (b * ns + s, 2 * heads + h)),
            pl.BlockSpec((1, LANES), lambda b, h, s: (0, h)),
            pl.BlockSpec((1, LANES), lambda b, h, s: (0, 0)),
            pl.BlockSpec(tab.shape, lambda b, h, s: (0, 0)),
            pl.BlockSpec(msk.shape, lambda b, h, s: (0, 0)),
        ],
        out_specs=pl.BlockSpec((ts, LANES), lambda b, h, s: (b * ns + s, h)),
        out_shape=jax.ShapeDtypeStruct((t, heads * LANES), BF16),
        scratch_shapes=[pltpu.VMEM((LANES, LANES), F32)],
        compiler_params=_params("parallel", "parallel", "arbitrary"),
        name="hgrn2_recurrence",
    )(qvg, z, qvg, qvg, lb.reshape(1, -1).astype(F32), g_norm.reshape(1, -1).astype(F32), tab, msk)


def _hgrn2(x, ln, w_in, lb, g_norm, w_o, batch, seq):
    d = x.shape[1]
    assert d == HG_HEADS * LANES
    w_qvg = jnp.concatenate([w_in[:, :d], w_in[:, 2 * d:]], axis=1).astype(BF16)
    w_z = w_in[:, d:2 * d].astype(BF16)
    qvg = _pro_matmul(x, ln, w_qvg, prologue="rms", out_dtype=BF16, name="hg_in_proj_qvg")
    z = _pro_matmul(x, ln, w_z, prologue="rms", out_dtype=F32, name="hg_in_proj_z")
    o = _hgrn2_core(qvg, z, lb, g_norm, batch, seq)
    return _matmul_res(o, w_o.astype(BF16), x, name="hg_out_proj")


def _s5_tables(a_re, a_im, b_re, b_im, c_re, c_im, d_skip, log_dt, nblk):
    L = S5_BLOCK
    G, P = a_re.shape
    N = b_re.shape[-1]
    a = lax.complex(a_re.astype(F32), a_im.astype(F32))
    dt_a = a * jnp.exp(log_dt.astype(F32))[:, None]
    a_bar = jnp.exp(dt_a)
    b_bar = ((a_bar - 1.0) / a)[:, :, None] * lax.complex(b_re.astype(F32), b_im.astype(F32))
    cc = lax.complex(c_re.astype(F32), c_im.astype(F32))
    pw = jnp.exp(jnp.arange(L + 1, dtype=F32)[:, None, None] * dt_a)
    kern = jnp.real(jnp.einsum("gnp,tgp,gpm->gtnm", cc, pw[:L], b_bar))
    kern = kern.at[:, 0].add(jax.vmap(jnp.diag)(d_skip.astype(F32)))
    s_idx = jnp.arange(L)[:, None]
    t_idx = jnp.arange(L)[None, :]
    lag = t_idx - s_idx
    toep = jnp.where((lag >= 0)[None, :, :, None, None], kern[:, jnp.clip(lag, 0, L - 1)], 0.0)
    toep = toep.transpose(0, 1, 4, 2, 3).reshape(G, L * N, L * N)
    wst = jnp.einsum("sgp,gpm->gsmp", pw[:L][::-1], b_bar)
    w_st = jnp.concatenate([jnp.real(wst), jnp.imag(wst)], axis=-1).reshape(G, L * N, 2 * P)
    co = jnp.einsum("gnp,tgp->gptn", cc, pw[1:])
    c_out = jnp.concatenate([jnp.real(co), -jnp.imag(co)], axis=1).reshape(G, 2 * P, L * N)
    nsteps = max(1, int(math.ceil(math.log2(nblk))))
    lam_k = jnp.exp((L * 2.0 ** jnp.arange(nsteps, dtype=F32))[None, :, None] * dt_a[:, None, :])
    lr, li = jnp.real(lam_k), jnp.imag(lam_k)
    lam = jnp.stack([jnp.concatenate([lr, lr], -1), jnp.concatenate([-li, li], -1)], axis=2)
    return toep.astype(BF16), w_st.astype(BF16), c_out.astype(BF16), lam.astype(F32)


def _s5_kernel(u_ref, toep_ref, wst_ref, cout_ref, lam_ref, y_ref, *, nsteps):
    L, N = S5_BLOCK, S5_GROUP
    nblk = u_ref.shape[0] // L
    gpl = LANES // N
    nchunk = L // gpl
    lane_grp = lax.broadcasted_iota(jnp.int32, (1, LANES), 1) // N

    def place(pieces, shifts):
        out = None
        for i, (p, s) in enumerate(zip(pieces, shifts)):
            r = p if s % gpl == 0 else pltpu.roll(p, (N * s) % LANES, axis=1)
            out = r if out is None else jnp.where(lane_grp == i, r, out)
        return out

    v = [u_ref[pl.ds(t, nblk, stride=L), :] for t in range(L)]
    blk = lax.broadcasted_iota(jnp.int32, (nblk, wst_ref.shape[-1]), 0)
    ys = []
    for g in range(gpl):
        ug = jnp.concatenate(
            [place([v[j * gpl + tau] for tau in range(gpl)], [tau - g for tau in range(gpl)])
             for j in range(nchunk)], axis=1).astype(BF16)
        z = jnp.dot(ug, wst_ref[g], preferred_element_type=F32)
        p2 = z.shape[1]
        h = z
        for k in range(nsteps):
            sh = 1 << k
            prev = jnp.where(blk >= sh, pltpu.roll(h, sh, axis=0), 0.0)
            lam = lam_ref[g, k]
            h = h + prev * lam[0:1, :] + pltpu.roll(prev, p2 // 2, axis=1) * lam[1:2, :]
        h0 = jnp.where(blk >= 1, pltpu.roll(h, 1, axis=0), 0.0)
        ys.append(jnp.dot(ug, toep_ref[g], preferred_element_type=F32)
                  + jnp.dot(h0.astype(BF16), cout_ref[g], preferred_element_type=F32))
    for t in range(L):
        j, tau = divmod(t, gpl)
        w = place([ys[g][:, j * LANES:(j + 1) * LANES] for g in range(gpl)], [g - tau for g in range(gpl)])
        y_ref[pl.ds(t, nblk, stride=L), :] = w


def _s5_core(u, toep, w_st, c_out, lam, batch, seq):
    t, d = u.shape
    gpl = LANES // S5_GROUP
    ln = toep.shape[-1]
    p2 = w_st.shape[-1]
    nsteps = lam.shape[1]
    return pl.pallas_call(
        functools.partial(_s5_kernel, nsteps=nsteps),
        grid=(d // LANES, batch),
        in_specs=[
            pl.BlockSpec((seq, LANES), lambda i, b: (b, i)),
            pl.BlockSpec((gpl, ln, ln), lambda i, b: (i, 0, 0)),
            pl.BlockSpec((gpl, ln, p2), lambda i, b: (i, 0, 0)),
            pl.BlockSpec((gpl, p2, ln), lambda i, b: (i, 0, 0)),
            pl.BlockSpec((gpl, nsteps, 2, p2), lambda i, b: (i, 0, 0, 0)),
        ],
        out_specs=pl.BlockSpec((seq, LANES), lambda i, b: (b, i)),
        out_shape=jax.ShapeDtypeStruct((t, d), F32),
        compiler_params=_params("parallel", "parallel"),
        name="s5_ssm",
    )(u, toep, w_st, c_out, lam)


def _s5(x, ln, w_in, a_re, a_im, b_re, b_im, c_re, c_im, d_skip, log_dt, w_out, batch, seq):
    nblk = seq // S5_BLOCK
    u = _pro_matmul(x, ln, w_in.astype(BF16), prologue="rms", out_dtype=F32, name="s5_in_proj")
    toep, w_st, c_out, lam = _s5_tables(a_re, a_im, b_re, b_im, c_re, c_im, d_skip, log_dt, nblk)
    y = _s5_core(u, toep, w_st, c_out, lam, batch, seq)
    return _s5_out(y, w_out.astype(BF16), x)


def _router_kernel(x_ref, g_ref, w_ref, h_ref, r_ref):
    h = _rms(x_ref[...], g_ref[...])
    h_ref[...] = h
    h0 = h.astype(BF16)
    h1 = (h - h0.astype(F32)).astype(BF16)
    w = w_ref[...]
    w0 = w.astype(BF16)
    w1 = (w - w0.astype(F32)).astype(BF16)
    logits = (jnp.dot(h0, w0, preferred_element_type=F32) + jnp.dot(h0, w1, preferred_element_type=F32)
              + jnp.dot(h1, w0, preferred_element_type=F32))
    lane = lax.broadcasted_iota(jnp.int32, logits.shape, 1).astype(F32)
    neg = -jnp.inf
    lg = jnp.where(lane < N_EXPERTS, logits, neg)
    m1 = jnp.max(lg, axis=-1, keepdims=True)
    i1 = jnp.min(jnp.where(lg == m1, lane, float(LANES)), axis=-1, keepdims=True)
    lg2 = jnp.where(lane == i1, neg, lg)
    m2 = jnp.max(lg2, axis=-1, keepdims=True)
    i2 = jnp.min(jnp.where(lg2 == m2, lane, float(LANES)), axis=-1, keepdims=True)
    e2 = jnp.exp(m2 - m1)
    g1 = 1.0 / (1.0 + e2)
    g2 = e2 / (1.0 + e2)
    out = jnp.where(lane == 0, g1, jnp.where(lane == 1, g2, jnp.where(lane == 2, i1, jnp.where(lane == 3, i2, 0.0))))
    r_ref[...] = out[:, :r_ref.shape[1]]


def _router(x, gain, w_router):
    m, d = x.shape
    tm = _tile(m, TM)
    w_pad = jnp.zeros((d, LANES), F32).at[:, :N_EXPERTS].set(w_router.astype(F32))
    return pl.pallas_call(
        _router_kernel,
        grid=(m // tm,),
        in_specs=[
            pl.BlockSpec((tm, d), lambda i: (i, 0)),
            pl.BlockSpec((1, d), lambda i: (0, 0)),
            pl.BlockSpec((d, LANES), lambda i: (0, 0)),
        ],
        out_specs=[pl.BlockSpec((tm, d), lambda i: (i, 0)), pl.BlockSpec((tm, 8), lambda i: (i, 0))],
        out_shape=[jax.ShapeDtypeStruct((m, d), F32), jax.ShapeDtypeStruct((m, 8), F32)],
        compiler_params=_params("parallel"),
        name="moe_router_top2",
    )(x, gain.reshape(1, d).astype(F32), w_pad)


def _moe_kernel(te_ref, ta_ref, tv_ref, idx_ref, h_hbm, wg_ref, wu_ref, wd_ref, y_hbm,
                x_ref, xb_ref, acc_ref, yb_ref, pend_ref, sem_g, sem_s):
    i = pl.program_id(0)
    f = pl.program_id(1)
    nt = pl.num_programs(0)
    last_f = pl.num_programs(1) - 1
    tm = xb_ref.shape[0]
    slot = lax.rem(i, 2)
    active = ta_ref[i] == 1
    nxt = jnp.minimum(i + 1, nt - 1)
    carry = active & (i + 1 < nt) & (ta_ref[nxt] == 1)
    n_dma_steps = tm // GATHER_CHUNK

    def start_gather(tile, sl):
        def body(o, c):
            base = pl.multiple_of(o * DMA_UNROLL, DMA_UNROLL)
            for j in range(DMA_UNROLL):
                r = base + j
                pltpu.make_async_copy(h_hbm.at[pl.ds(idx_ref[tile, r], 1)], x_ref.at[sl, pl.ds(r, 1)],
                                      sem_g.at[sl]).start()
            return c

        lax.fori_loop(0, tm // DMA_UNROLL, body, 0)

    def wait_gather(sl):
        pltpu.make_async_copy(h_hbm.at[pl.ds(0, tm)], x_ref.at[sl], sem_g.at[sl]).wait()

    def scatter_row(r):
        pltpu.make_async_copy(yb_ref.at[pl.ds(r, 1)], y_hbm.at[pl.ds(idx_ref[i, tm + r], 1)], sem_s).start()

    def start_scatter():
        @pl.when(tv_ref[i] == tm)
        def _():
            def body(o, c):
                base = pl.multiple_of(o * DMA_UNROLL, DMA_UNROLL)
                for j in range(DMA_UNROLL):
                    scatter_row(base + j)
                return c

            lax.fori_loop(0, tm // DMA_UNROLL, body, 0)

        @pl.when(tv_ref[i] < tm)
        def _():
            def body(r, c):
                scatter_row(r)
                return c

            lax.fori_loop(0, tv_ref[i], body, 0)

    def wait_scatter():
        @pl.when(pend_ref[0] > 0)
        def _():
            nv = tv_ref[jnp.maximum(pend_ref[0] - 1, 0)]
            for bit in reversed(range(tm.bit_length())):
                n = 1 << bit

                @pl.when((nv & n) != 0)
                def _():
                    pltpu.make_async_copy(yb_ref.at[pl.ds(0, n)], y_hbm.at[pl.ds(0, n)], sem_s).wait()

            pend_ref[0] = 0

    @pl.when((i == 0) & (f == 0))
    def _():
        pend_ref[0] = 0

        @pl.when(active)
        def _():
            start_gather(0, 0)

    @pl.when(active & (f == 0))
    def _():
        wait_gather(slot)
        xb_ref[...] = x_ref[slot].astype(BF16)
        acc_ref[...] = jnp.zeros_like(acc_ref)

    @pl.when(carry & (f < n_dma_steps))
    def _():
        base = pl.multiple_of(f * GATHER_CHUNK, GATHER_CHUNK)
        for j in range(GATHER_CHUNK):
            r = base + j
            pltpu.make_async_copy(h_hbm.at[pl.ds(idx_ref[nxt, r], 1)], x_ref.at[1 - slot, pl.ds(r, 1)],
                                  sem_g.at[1 - slot]).start()
        acc_ref[...] += _swiglu_step(xb_ref[...], wg_ref[...], wu_ref[...], wd_ref[...])

    @pl.when(active & jnp.logical_not(carry & (f < n_dma_steps)))
    def _():
        acc_ref[...] += _swiglu_step(xb_ref[...], wg_ref[...], wu_ref[...], wd_ref[...])

    @pl.when(active & (f == last_f))
    def _():
        wait_scatter()
        yb_ref[...] = acc_ref[...]
        start_scatter()
        pend_ref[0] = i + 1

    @pl.when((i == nt - 1) & (f == last_f))
    def _():
        wait_scatter()


def _moe_experts(h, idx, tile_expert, tile_active, tile_valid, w_gu, w_down, layer):
    n_tok, d = h.shape
    n_tiles, tm = idx.shape[0], idx.shape[1] // 2
    ff = w_down.shape[2]
    tf = _tile(ff, TF, LANES)
    nf = ff // tf
    assert tm % GATHER_CHUNK == 0 and tm // GATHER_CHUNK <= nf and tm % DMA_UNROLL == 0

    def fe(f, i, ta):
        return jnp.where(ta[i] == 1, f, nf - 1)

    grid_spec = pltpu.PrefetchScalarGridSpec(
        num_scalar_prefetch=4,
        grid=(n_tiles, nf),
        in_specs=[
            pl.BlockSpec(memory_space=pl.ANY),
            pl.BlockSpec((None, None, d, tf), lambda i, f, te, ta, tv, ix: (layer, te[i], 0, fe(f, i, ta))),
            pl.BlockSpec((None, None, d, tf), lambda i, f, te, ta, tv, ix: (layer, te[i], 0, fe(f, i, ta) + nf)),
            pl.BlockSpec((None, None, tf, d), lambda i, f, te, ta, tv, ix: (layer, te[i], fe(f, i, ta), 0)),
        ],
        out_specs=pl.BlockSpec(memory_space=pl.ANY),
        scratch_shapes=[
            pltpu.VMEM((2, tm, d), F32),
            pltpu.VMEM((tm, d), BF16),
            pltpu.VMEM((tm, d), F32),
            pltpu.VMEM((tm, d), F32),
            pltpu.SMEM((1,), jnp.int32),
            pltpu.SemaphoreType.DMA((2,)),
            pltpu.SemaphoreType.DMA,
        ],
    )
    return pl.pallas_call(
        _moe_kernel,
        grid_spec=grid_spec,
        out_shape=jax.ShapeDtypeStruct((TOP_K * n_tok, d), F32),
        compiler_params=_params("arbitrary", "arbitrary"),
        name="moe_grouped_swiglu",
    )(tile_expert, tile_active, tile_valid, idx, h, w_gu, w_gu, w_down)


def _combine_kernel(x_ref, a_ref, b_ref, r_ref, o_ref):
    r = r_ref[...]
    o_ref[...] = x_ref[...] + (r[:, 0:1] * a_ref[...] + r[:, 1:2] * b_ref[...])


def _combine(x, y, route):
    m, d = x.shape
    tm = _tile(m, TM)
    nb = m // tm
    return pl.pallas_call(
        _combine_kernel,
        grid=(nb,),
        in_specs=[
            pl.BlockSpec((tm, d), lambda i: (i, 0)),
            pl.BlockSpec((tm, d), lambda i: (i, 0)),
            pl.BlockSpec((tm, d), lambda i: (i + nb, 0)),
            pl.BlockSpec((tm, route.shape[1]), lambda i: (i, 0)),
        ],
        out_specs=pl.BlockSpec((tm, d), lambda i: (i, 0)),
        out_shape=jax.ShapeDtypeStruct((m, d), F32),
        compiler_params=_params("parallel"),
        name="moe_combine",
    )(x, y, y, route)


def _route_metadata(route, n_tok, tm):
    experts = route[:, TOP_K:2 * TOP_K].T.reshape(-1).astype(jnp.int32)
    n_pairs = TOP_K * n_tok
    onehot = (experts[:, None] == jnp.arange(N_EXPERTS)[None, :]).astype(jnp.int32)
    rank = jnp.take_along_axis(jnp.cumsum(onehot, axis=0), experts[:, None], axis=1)[:, 0] - 1
    counts = jnp.sum(onehot, axis=0)
    padded = ((counts + tm - 1) // tm) * tm
    ends = jnp.cumsum(padded)
    starts = ends - padded
    rows = starts[experts] + rank
    n_rows = n_pairs + N_EXPERTS * tm
    n_rows = (n_rows // tm) * tm
    dest = jnp.full((n_rows,), -1, jnp.int32).at[rows].set(jnp.arange(n_pairs, dtype=jnp.int32))
    src = jnp.where(dest < 0, 0, dest % n_tok)
    idx = jnp.concatenate([src.reshape(-1, tm), jnp.maximum(dest, 0).reshape(-1, tm)], axis=1)
    tile_start = jnp.arange(n_rows // tm, dtype=jnp.int32) * tm
    tile_expert = jnp.sum((tile_start[:, None] >= ends[None, :]).astype(jnp.int32), axis=1)
    tile_expert = jnp.minimum(tile_expert, N_EXPERTS - 1)
    tile_active = (tile_start < ends[-1]).astype(jnp.int32)
    tile_valid = jnp.clip((starts + counts)[tile_expert] - tile_start, 0, tm).astype(jnp.int32) * tile_active
    return idx, tile_expert, tile_active, tile_valid


def _moe(x, gain, w_router, w_gu, w_down, layer):
    n_tok = x.shape[0]
    h, route = _router(x, gain, w_router)
    tm = min(MOE_TM, n_tok)
    idx, tile_expert, tile_active, tile_valid = _route_metadata(route, n_tok, tm)
    y = _moe_experts(h, idx, tile_expert, tile_active, tile_valid, w_gu, w_down, layer)
    return _combine(x, y, route)


def _final_norm_kernel(x_ref, g_ref, o_ref):
    o_ref[...] = _rms(x_ref[...], g_ref[...])


def _final_norm(x, gain):
    m, d = x.shape
    tm = _tile(m, TM)
    return pl.pallas_call(
        _final_norm_kernel,
        grid=(m // tm,),
        in_specs=[pl.BlockSpec((tm, d), lambda i: (i, 0)), pl.BlockSpec((1, d), lambda i: (0, 0))],
        out_specs=pl.BlockSpec((tm, d), lambda i: (i, 0)),
        out_shape=jax.ShapeDtypeStruct((m, d), F32),
        compiler_params=_params("parallel"),
        name="final_rmsnorm",
    )(x, gain.reshape(1, d).astype(F32))


def kernel(x, ln_mix, ln_ffn, ln_final, mla_w_in, mla_q_norm, mla_kv_norm, mla_w_uq, mla_w_ukv, mla_w_o, hg_w_in, hg_lower_bound, hg_g_norm, hg_w_o, s5_w_in, s5_a_re, s5_a_im, s5_b_re, s5_b_im, s5_c_re, s5_c_im, s5_d, s5_log_dt, s5_w_out, ffn_w_gu, ffn_w_down, moe_w_router, moe_w_gu, moe_w_down):
    batch, seq, d = x.shape
    depth = ln_mix.shape[0]
    lb_w = jax.nn.softmax(hg_lower_bound.astype(F32), axis=0)
    lower_bounds = jnp.cumsum(lb_w, axis=0) - lb_w[0]
    xt = x.reshape(batch * seq, d).astype(F32)
    ffn_gu, ffn_down = ffn_w_gu.astype(BF16), ffn_w_down.astype(BF16)
    moe_gu, moe_down = moe_w_gu.astype(BF16), moe_w_down.astype(BF16)
    for i in range(depth):
        m, j = i % N_MIXERS, i // N_MIXERS
        if m == 0:
            xt = _mla(xt, ln_mix[i], mla_w_in[j], mla_q_norm[j], mla_kv_norm[j], mla_w_uq[j], mla_w_ukv[j],
                      mla_w_o[j], batch, seq)
        elif m == 1:
            xt = _hgrn2(xt, ln_mix[i], hg_w_in[j], lower_bounds[i], hg_g_norm[j], hg_w_o[j], batch, seq)
        else:
            xt = _s5(xt, ln_mix[i], s5_w_in[j], s5_a_re[j], s5_a_im[j], s5_b_re[j], s5_b_im[j], s5_c_re[j],
                     s5_c_im[j], s5_d[j], s5_log_dt[j], s5_w_out[j], batch, seq)
        f = i // 2
        if i % 2 == 0:
            xt = _ffn(xt, ln_ffn[i], ffn_gu, ffn_down, f)
        else:
            xt = _moe(xt, ln_ffn[i], moe_w_router[f], moe_gu, moe_down, f)
    return _final_norm(xt, ln_final).reshape(batch, seq, d)
```

```python
import functools
import math

import numpy as np
import jax
import jax.numpy as jnp
from jax import lax
from jax.experimental import pallas as pl
from jax.experimental.pallas import tpu as pltpu

F32 = jnp.float32
BF16 = jnp.bfloat16
EPS = 1e-6
N_MIXERS = 3

MLA_HEADS = 16
MLA_Q_RANK = 512
MLA_KV_RANK = 512
MLA_NOPE = 128
MLA_ROPE = 64
MLA_V = 128
ROPE_THETA = 10000.0
HG_HEADS = 16
HG_CHUNK = 64
S5_GROUP = 16
S5_STATE = 64
S5_BLOCK = 32
N_EXPERTS = 8
TOP_K = 2

LANES = 128
VMEM_LIMIT = 56 * 1024 * 1024
TILE_VMEM_BUDGET = 40 * 1024 * 1024
RESIDENT_WEIGHT_BYTES = 8 * 1024 * 1024

TM = 512
TN = 1024
TF = 512
ATT_TILE = 1024
ATT_GROUPS = 4
ATT_HEADS = 4
HG_TILE = 512
MOE_TM = 512
DMA_UNROLL = 8
GATHER_CHUNK = 64


def _tile(n, t, step=8):
    if n % step:
        return n
    best = step
    for c in range(step, min(n, t) + 1, step):
        if n % c == 0:
            best = c
    return best


def _matmul_tiles(m, k, n, vmem_bytes):
    tn = n if 2 * k * n <= RESIDENT_WEIGHT_BYTES else _tile(n, TN, LANES)
    for tm in (_tile(m, 2 * TM), _tile(m, TM)):
        if vmem_bytes(tm, tn) <= TILE_VMEM_BUDGET:
            break
    return tm, tn


def _params(*sem):
    return pltpu.CompilerParams(dimension_semantics=sem, vmem_limit_bytes=VMEM_LIMIT)


def _rms(x, g):
    return x * lax.rsqrt(jnp.mean(x * x, axis=-1, keepdims=True) + EPS) * g


def _gelu_tanh(y):
    c = math.sqrt(2.0 / math.pi)
    return 0.5 * y * (1.0 + jnp.tanh(c * (y + 0.044715 * (y * y * y))))


def _sigmoid(x):
    return 1.0 / (1.0 + jnp.exp(-x))


def _pro_matmul_kernel(x_ref, g_ref, w_ref, o_ref, h_ref, *, prologue):
    @pl.when(pl.program_id(1) == 0)
    def _():
        x = x_ref[...].astype(F32)
        if prologue == "rms":
            x = _rms(x, g_ref[...])
        h_ref[...] = x.astype(BF16)

    o_ref[...] = jnp.dot(h_ref[...], w_ref[...], preferred_element_type=F32).astype(o_ref.dtype)


def _pro_matmul(x, gain, w, *, prologue, out_dtype, x_col_block=0, name):
    m = x.shape[0]
    k, n = w.shape
    xb, ob = x.dtype.itemsize, jnp.dtype(out_dtype).itemsize
    tm, tn = _matmul_tiles(m, k, n, lambda tm, tn: 2 * tm * k * xb + 2 * tm * k + 4 * k * tn + 2 * tm * tn * ob)
    return pl.pallas_call(
        functools.partial(_pro_matmul_kernel, prologue=prologue),
        grid=(m // tm, n // tn),
        in_specs=[
            pl.BlockSpec((tm, k), lambda i, j: (i, x_col_block)),
            pl.BlockSpec((1, k), lambda i, j: (0, 0)),
            pl.BlockSpec((k, tn), lambda i, j: (0, j)),
        ],
        out_specs=pl.BlockSpec((tm, tn), lambda i, j: (i, j)),
        out_shape=jax.ShapeDtypeStruct((m, n), out_dtype),
        scratch_shapes=[pltpu.VMEM((tm, k), BF16)],
        compiler_params=_params("parallel", "arbitrary"),
        name=name,
    )(x, gain.reshape(1, k).astype(F32), w)


def _matmul_res_kernel(a_ref, w_ref, r_ref, o_ref):
    o_ref[...] = r_ref[...] + jnp.dot(a_ref[...], w_ref[...], preferred_element_type=F32)


def _matmul_res(a, w, res, *, name):
    m, k = a.shape
    n = w.shape[1]
    tm, tn = _matmul_tiles(m, k, n, lambda tm, tn: 4 * tm * k + 4 * k * tn + 16 * tm * tn)
    return pl.pallas_call(
        _matmul_res_kernel,
        grid=(m // tm, n // tn),
        in_specs=[
            pl.BlockSpec((tm, k), lambda i, j: (i, 0)),
            pl.BlockSpec((k, tn), lambda i, j: (0, j)),
            pl.BlockSpec((tm, tn), lambda i, j: (i, j)),
        ],
        out_specs=pl.BlockSpec((tm, tn), lambda i, j: (i, j)),
        out_shape=jax.ShapeDtypeStruct((m, n), F32),
        compiler_params=_params("parallel", "parallel"),
        name=name,
    )(a, w, res)


def _s5_out_kernel(y_ref, wv_ref, wg_ref, r_ref, o_ref, h_ref):
    @pl.when(pl.program_id(1) == 0)
    def _():
        h_ref[...] = _gelu_tanh(y_ref[...].astype(F32)).astype(BF16)

    h = h_ref[...]
    val = jnp.dot(h, wv_ref[...], preferred_element_type=F32)
    gate = jnp.dot(h, wg_ref[...], preferred_element_type=F32)
    o_ref[...] = r_ref[...] + val * _sigmoid(gate)


def _s5_out(y, w_out, res):
    m, k = y.shape
    n = w_out.shape[1] // 2
    tm, tn = _tile(m, TM), _tile(n, TN, LANES)
    nb = n // tn
    return pl.pallas_call(
        _s5_out_kernel,
        grid=(m // tm, nb),
        in_specs=[
            pl.BlockSpec((tm, k), lambda i, j: (i, 0)),
            pl.BlockSpec((k, tn), lambda i, j: (0, j)),
            pl.BlockSpec((k, tn), lambda i, j: (0, j + nb)),
            pl.BlockSpec((tm, tn), lambda i, j: (i, j)),
        ],
        out_specs=pl.BlockSpec((tm, tn), lambda i, j: (i, j)),
        out_shape=jax.ShapeDtypeStruct((m, n), F32),
        scratch_shapes=[pltpu.VMEM((tm, k), BF16)],
        compiler_params=_params("parallel", "arbitrary"),
        name="s5_out_glu",
    )(y, w_out, w_out, res)


def _swiglu_step(h, wg, wu, wd):
    g = jnp.dot(h, wg, preferred_element_type=F32)
    u = jnp.dot(h, wu, preferred_element_type=F32)
    a = (g * _sigmoid(g) * u).astype(BF16)
    return jnp.dot(a, wd, preferred_element_type=F32)


def _ffn_kernel(x_ref, g_ref, wg_ref, wu_ref, wd_ref, o_ref, h_ref, acc_ref):
    f = pl.program_id(1)

    @pl.when(f == 0)
    def _():
        h_ref[...] = _rms(x_ref[...], g_ref[...]).astype(BF16)
        acc_ref[...] = jnp.zeros_like(acc_ref)

    acc_ref[...] += _swiglu_step(h_ref[...], wg_ref[...], wu_ref[...], wd_ref[...])

    @pl.when(f == pl.num_programs(1) - 1)
    def _():
        o_ref[...] = x_ref[...] + acc_ref[...]


def _ffn(x, gain, w_gu, w_down, layer):
    m, d = x.shape
    ff = w_down.shape[1]
    tm, tf = _tile(m, TM), _tile(ff, TF, LANES)
    nf = ff // tf
    return pl.pallas_call(
        _ffn_kernel,
        grid=(m // tm, nf),
        in_specs=[
            pl.BlockSpec((tm, d), lambda i, f: (i, 0)),
            pl.BlockSpec((1, d), lambda i, f: (0, 0)),
            pl.BlockSpec((None, d, tf), lambda i, f: (layer, 0, f)),
            pl.BlockSpec((None, d, tf), lambda i, f: (layer, 0, f + nf)),
            pl.BlockSpec((None, tf, d), lambda i, f: (layer, f, 0)),
        ],
        out_specs=pl.BlockSpec((tm, d), lambda i, f: (i, 0)),
        out_shape=jax.ShapeDtypeStruct((m, d), F32),
        scratch_shapes=[pltpu.VMEM((tm, d), BF16), pltpu.VMEM((tm, d), F32)],
        compiler_params=_params("parallel", "arbitrary"),
        name="ffn_swiglu",
    )(x, gain.reshape(1, d).astype(F32), w_gu, w_gu, w_down)


def _rope_table(seq):
    half = MLA_ROPE // 2
    inv = ROPE_THETA ** (-jnp.arange(half, dtype=F32) / half)
    ang = jnp.arange(seq, dtype=F32)[:, None] * inv
    cos, sin = jnp.cos(ang), jnp.sin(ang)
    return jnp.concatenate([cos, cos, sin, sin], axis=-1)


def _rot_cols(w):
    half = w.shape[-1] // 2
    return jnp.concatenate([-w[..., half:], w[..., :half]], axis=-1)


def _rope_slab(slab, cs):
    prod = slab * cs
    return prod + pltpu.roll(prod, MLA_ROPE, axis=1)


def _krope_kernel(p_ref, cs_ref, o_ref):
    r = _rope_slab(p_ref[...].astype(F32), cs_ref[...])
    lane = lax.broadcasted_iota(jnp.int32, r.shape, 1)
    o_ref[...] = jnp.where(lane < MLA_ROPE, r, 0.0).astype(o_ref.dtype)


def _krope(proj, cs, seq, col_block):
    t = proj.shape[0]
    tm = _tile(seq, TM)
    ns = seq // tm
    return pl.pallas_call(
        _krope_kernel,
        grid=(t // tm,),
        in_specs=[
            pl.BlockSpec((tm, LANES), lambda i: (i, col_block)),
            pl.BlockSpec((tm, LANES), lambda i: (i % ns, 0)),
        ],
        out_specs=pl.BlockSpec((tm, LANES), lambda i: (i, 0)),
        out_shape=jax.ShapeDtypeStruct((t, LANES), BF16),
        compiler_params=_params("parallel"),
        name="mla_k_rope",
    )(proj, cs)


def _attn_kernel(qi_ref, ki_ref, q_ref, cs_ref, kv_ref, kr_ref, o_ref, qc_ref, m_ref, acc_ref):
    n_heads = qc_ref.shape[0]
    s_id = pl.program_id(2)
    qi = qi_ref[s_id]
    ki = ki_ref[s_id]
    hw = 2 * LANES

    @pl.when(ki == 0)
    def _():
        for hh in range(n_heads):
            q = q_ref[:, hh * hw:(hh + 1) * hw]
            qr = _rope_slab(q[:, LANES:].astype(F32), cs_ref[...]).astype(BF16)
            qc_ref[hh] = jnp.concatenate([q[:, :LANES], qr], axis=1)
        m_ref[...] = jnp.full_like(m_ref, -jnp.inf)
        acc_ref[...] = jnp.zeros_like(acc_ref)

    lane = lax.broadcasted_iota(jnp.int32, kr_ref.shape, 1)
    ones_col = jnp.where(lane == 0, 1.0, 0.0).astype(BF16)
    rq = q_ref.shape[0] // ATT_GROUPS

    def sweep(masked):
        for hh in range(n_heads):
            kc = jnp.concatenate([kv_ref[:, hh * hw:hh * hw + LANES], kr_ref[...]], axis=1)
            ve = jnp.concatenate([kv_ref[:, hh * hw + LANES:(hh + 1) * hw], ones_col], axis=1)

            def scores(g):
                return lax.dot_general(qc_ref[hh, g * rq:(g + 1) * rq, :], kc, (((1,), (1,)), ((), ())),
                                       preferred_element_type=F32)

            def update(g, sc):
                rows = slice(g * rq, (g + 1) * rq)
                if masked:
                    row = lax.broadcasted_iota(jnp.int32, sc.shape, 0) + g * rq
                    col = lax.broadcasted_iota(jnp.int32, sc.shape, 1)
                    sc = jnp.where(col <= row, sc, -jnp.inf)
                m_prev = m_ref[hh, rows, :]
                m_new = jnp.maximum(m_prev, jnp.max(sc, axis=-1, keepdims=True))
                alpha = jnp.exp2(m_prev - m_new)
                p = jnp.exp2(sc - m_new)
                acc_ref[hh, rows, :] = alpha * acc_ref[hh, rows, :] + jnp.dot(p.astype(BF16), ve,
                                                                              preferred_element_type=F32)
                m_ref[hh, rows, :] = m_new

            nxt = scores(0)
            for g in range(ATT_GROUPS):
                cur = nxt
                if g + 1 < ATT_GROUPS:
                    nxt = scores(g + 1)
                update(g, cur)

    @pl.when(ki < qi)
    def _():
        sweep(False)

    @pl.when(ki == qi)
    def _():
        sweep(True)
        for hh in range(n_heads):
            a = acc_ref[hh]
            o_ref[:, hh * LANES:(hh + 1) * LANES] = (a[:, :LANES] / a[:, LANES:LANES + 1]).astype(o_ref.dtype)


def _attention(q_ext, kv_ext, k_rope, cs, batch, seq):
    t = q_ext.shape[0]
    heads = q_ext.shape[1] // (2 * LANES)
    hps = math.gcd(heads, ATT_HEADS)
    hw = hps * 2 * LANES
    tq = _tile(seq, ATT_TILE)
    nq = seq // tq
    steps = [(i, j) for i in range(nq) for j in range(i + 1)]
    qi_tab = jnp.asarray([s[0] for s in steps], jnp.int32)
    ki_tab = jnp.asarray([s[1] for s in steps], jnp.int32)
    grid_spec = pltpu.PrefetchScalarGridSpec(
        num_scalar_prefetch=2,
        grid=(batch, heads // hps, len(steps)),
        in_specs=[
            pl.BlockSpec((tq, hw), lambda b, h, s, qi, ki: (b * nq + qi[s], h)),
            pl.BlockSpec((tq, LANES), lambda b, h, s, qi, ki: (qi[s], 0)),
            pl.BlockSpec((tq, hw), lambda b, h, s, qi, ki: (b * nq + ki[s], h)),
            pl.BlockSpec((tq, LANES), lambda b, h, s, qi, ki: (b * nq + ki[s], 0)),
        ],
        out_specs=pl.BlockSpec((tq, hps * LANES), lambda b, h, s, qi, ki: (b * nq + qi[s], h)),
        scratch_shapes=[
            pltpu.VMEM((hps, tq, 2 * LANES), BF16),
            pltpu.VMEM((hps, tq, 1), F32),
            pltpu.VMEM((hps, tq, 2 * LANES), F32),
        ],
    )
    return pl.pallas_call(
        _attn_kernel,
        grid_spec=grid_spec,
        out_shape=jax.ShapeDtypeStruct((t, heads * LANES), BF16),
        compiler_params=_params("parallel", "parallel", "arbitrary"),
        name="mla_flash_attention",
    )(qi_tab, ki_tab, q_ext, cs, kv_ext, k_rope)


def _mla(x, ln, w_in, q_norm, kv_norm, w_uq, w_ukv, w_o, batch, seq):
    heads = MLA_HEADS
    assert MLA_NOPE == LANES and MLA_V == LANES and 2 * MLA_ROPE == LANES
    assert MLA_Q_RANK == MLA_KV_RANK
    rank = MLA_Q_RANK
    w_kr = w_in[:, 2 * rank:]
    w_in_ext = jnp.concatenate([w_in, _rot_cols(w_kr)], axis=1).astype(BF16)
    proj = _pro_matmul(x, ln, w_in_ext, prologue="rms", out_dtype=F32, name="mla_in_proj")
    scale = (MLA_NOPE + MLA_ROPE) ** -0.5 * math.log2(math.e)
    wq = w_uq.reshape(rank, heads, MLA_NOPE + MLA_ROPE) * scale
    wq_ext = jnp.concatenate([wq, _rot_cols(wq[..., MLA_NOPE:])], axis=-1)
    wq_ext = wq_ext.reshape(rank, heads * 2 * LANES).astype(BF16)
    q_ext = _pro_matmul(proj, q_norm, wq_ext, prologue="rms", out_dtype=BF16, x_col_block=0, name="mla_q_up")
    kv_ext = _pro_matmul(proj, kv_norm, w_ukv.astype(BF16), prologue="rms", out_dtype=BF16, x_col_block=1,
                         name="mla_kv_up")
    cs = _rope_table(seq)
    k_rope = _krope(proj, cs, seq, col_block=2 * rank // LANES)
    o = _attention(q_ext, kv_ext, k_rope, cs, batch, seq)
    return _matmul_res(o, w_o.astype(BF16), x, name="mla_out_proj")


def _hg_tables(c):
    levels = int(math.log2(c))
    assert 2 ** levels == c
    t = np.arange(c)[:, None]
    u = np.arange(c)[None, :]
    mats = []
    for l in range(1, levels + 1):
        mid = ((t >> l) << l) + (1 << (l - 1))
        upper = t >= mid
        mats.append(np.where(upper, (u >= mid) & (u <= t), (u > t) & (u <= mid - 1)))
    mats.append(u <= t)
    mats.append(u > t)
    masks = [t == u]
    for l in range(1, levels + 1):
        masks.append(((t >> l) == (u >> l)) & (((t >> (l - 1)) & 1) == 1) & (((u >> (l - 1)) & 1) == 0))
    return (np.concatenate(mats, axis=0).astype(np.float32), np.concatenate(masks, axis=0).astype(np.float32),
            levels)


def _split3(x):
    hi = x.astype(BF16)
    r = x - hi.astype(F32)
    mid = r.astype(BF16)
    lo = (r - mid.astype(F32)).astype(BF16)
    return hi, mid, lo


def _hgrn2_kernel(q_ref, z_ref, v_ref, g_ref, lb_ref, gn_ref, tab_ref, msk_ref, o_ref, st_ref, *, chunk, levels):
    c = chunk
    n_chunks = q_ref.shape[0] // c
    dk = q_ref.shape[1]

    @pl.when(pl.program_id(2) == 0)
    def _():
        st_ref[...] = jnp.zeros_like(st_ref)

    lb = lb_ref[...]
    log_lb = jnp.log(lb)
    log_1mlb = jnp.log1p(-lb)
    z = z_ref[...]
    q = q_ref[...].astype(F32)
    e = jnp.exp(-jnp.abs(z))
    log_sig = jnp.minimum(z, 0.0) - jnp.log1p(e)
    bterm = log_1mlb + log_sig
    lf = jnp.maximum(log_lb, bterm) + jnp.log1p(jnp.exp(-jnp.abs(log_lb - bterm)))
    k = (1.0 - lb) * jnp.where(z >= 0, e, 1.0) / (1.0 + e)

    def chunks_on_lanes(x):
        return jnp.concatenate([x[i * c:(i + 1) * c] for i in range(n_chunks)], axis=1)

    tab = tab_ref[...]
    p0, p1, p2 = _split3(lf)
    sums = (jnp.dot(tab, chunks_on_lanes(p0), preferred_element_type=F32)
            + jnp.dot(tab, chunks_on_lanes(p1), preferred_element_type=F32)
            + jnp.dot(tab, chunks_on_lanes(p2), preferred_element_type=F32))
    dec = jnp.exp(sums)
    diag = jnp.sum(q * k, axis=-1, keepdims=True)
    o_intra, q_in, upds, d_last = [], [], [], []
    for i in range(n_chunks):
        rows = slice(i * c, (i + 1) * c)
        qi, ki, vi = q[rows], k[rows], v_ref[rows, :]
        di = dec[:, i * dk:(i + 1) * dk]
        scores = msk_ref[0:c, :] * diag[rows]
        for l in range(1, levels + 1):
            d = di[(l - 1) * c:l * c]
            sl_ = lax.dot_general((qi * d).astype(BF16), (ki * d).astype(BF16), (((1,), (1,)), ((), ())),
                                  preferred_element_type=F32)
            scores = scores + msk_ref[l * c:(l + 1) * c, :] * sl_
        d_pre = di[levels * c:(levels + 1) * c]
        d_suf = di[(levels + 1) * c:(levels + 2) * c]
        o_intra.append(jnp.dot(scores.astype(BF16), vi, preferred_element_type=F32))
        q_in.append((qi * d_pre).astype(BF16))
        upds.append(lax.dot_general(vi, (ki * d_suf).astype(BF16), (((0,), (0,)), ((), ())),
                                    preferred_element_type=F32))
        d_last.append(d_pre[c - 1:c, :])
    st = st_ref[...]
    outs = []
    for i in range(n_chunks):
        outs.append(o_intra[i] + lax.dot_general(q_in[i], st.astype(BF16), (((1,), (1,)), ((), ())),
                                                 preferred_element_type=F32))
        st = st * d_last[i] + upds[i]
    st_ref[...] = st
    o = jnp.concatenate(outs, axis=0)
    gt = g_ref[...].astype(F32)
    o_ref[...] = (_rms(o, gn_ref[...]) * (gt * _sigmoid(gt))).astype(o_ref.dtype)


def _hgrn2_core(qvg, z, lb, g_norm, batch, seq):
    t = z.shape[0]
    heads = z.shape[1] // LANES
    ts = _tile(seq, HG_TILE)
    ns = seq // ts
    c = min(HG_CHUNK, ts)
    tab, msk, levels = _hg_tables(c)
    tab = jnp.asarray(tab, BF16)
    msk = jnp.asarray(msk, F32)
    return pl.pallas_call(
        functools.partial(_hgrn2_kernel, chunk=c, levels=levels),
        grid=(batch, heads, ns),
        in_specs=[
            pl.BlockSpec((ts, LANES), lambda b, h, s: (b * ns + s, h)),
            pl.BlockSpec((ts, LANES), lambda b, h, s: (b * ns + s, h)),
            pl.BlockSpec((ts, LANES), lambda b, h, s: (b * ns + s, heads + h)),
            pl.BlockSpec((ts, LANES), lambda b, h, s: (b * ns + s, 2 * heads + h)),
            pl.BlockSpec((1, LANES), lambda b, h, s: (0, h)),
            pl.BlockSpec((1, LANES), lambda b, h, s: (0, 0)),
            pl.BlockSpec(tab.shape, lambda b, h, s: (0, 0)),
            pl.BlockSpec(msk.shape, lambda b, h, s: (0, 0)),
        ],
        out_specs=pl.BlockSpec((ts, LANES), lambda b, h, s: (b * ns + s, h)),
        out_shape=jax.ShapeDtypeStruct((t, heads * LANES), BF16),
        scratch_shapes=[pltpu.VMEM((LANES, LANES), F32)],
        compiler_params=_params("parallel", "parallel", "arbitrary"),
        name="hgrn2_recurrence",
    )(qvg, z, qvg, qvg, lb.reshape(1, -1).astype(F32), g_norm.reshape(1, -1).astype(F32), tab, msk)


def _hgrn2(x, ln, w_in, lb, g_norm, w_o, batch, seq):
    d = x.shape[1]
    assert d == HG_HEADS * LANES
    w_qvg = jnp.concatenate([w_in[:, :d], w_in[:, 2 * d:]], axis=1).astype(BF16)
    w_z = w_in[:, d:2 * d].astype(BF16)
    qvg = _pro_matmul(x, ln, w_qvg, prologue="rms", out_dtype=BF16, name="hg_in_proj_qvg")
    z = _pro_matmul(x, ln, w_z, prologue="rms", out_dtype=F32, name="hg_in_proj_z")
    o = _hgrn2_core(qvg, z, lb, g_norm, batch, seq)
    return _matmul_res(o, w_o.astype(BF16), x, name="hg_out_proj")


def _s5_tables(a_re, a_im, b_re, b_im, c_re, c_im, d_skip, log_dt, nblk):
    L = S5_BLOCK
    G, P = a_re.shape
    N = b_re.shape[-1]
    a = lax.complex(a_re.astype(F32), a_im.astype(F32))
    dt_a = a * jnp.exp(log_dt.astype(F32))[:, None]
    a_bar = jnp.exp(dt_a)
    b_bar = ((a_bar - 1.0) / a)[:, :, None] * lax.complex(b_re.astype(F32), b_im.astype(F32))
    cc = lax.complex(c_re.astype(F32), c_im.astype(F32))
    pw = jnp.exp(jnp.arange(L + 1, dtype=F32)[:, None, None] * dt_a)
    kern = jnp.real(jnp.einsum("gnp,tgp,gpm->gtnm", cc, pw[:L], b_bar))
    kern = kern.at[:, 0].add(jax.vmap(jnp.diag)(d_skip.astype(F32)))
    s_idx = jnp.arange(L)[:, None]
    t_idx = jnp.arange(L)[None, :]
    lag = t_idx - s_idx
    toep = jnp.where((lag >= 0)[None, :, :, None, None], kern[:, jnp.clip(lag, 0, L - 1)], 0.0)
    toep = toep.transpose(0, 1, 4, 2, 3).reshape(G, L * N, L * N)
    wst = jnp.einsum("sgp,gpm->gsmp", pw[:L][::-1], b_bar)
    w_st = jnp.concatenate([jnp.real(wst), jnp.imag(wst)], axis=-1).reshape(G, L * N, 2 * P)
    co = jnp.einsum("gnp,tgp->gptn", cc, pw[1:])
    c_out = jnp.concatenate([jnp.real(co), -jnp.imag(co)], axis=1).reshape(G, 2 * P, L * N)
    nsteps = max(1, int(math.ceil(math.log2(nblk))))
    lam_k = jnp.exp((L * 2.0 ** jnp.arange(nsteps, dtype=F32))[None, :, None] * dt_a[:, None, :])
    lr, li = jnp.real(lam_k), jnp.imag(lam_k)
    lam = jnp.stack([jnp.concatenate([lr, lr], -1), jnp.concatenate([-li, li], -1)], axis=2)
    return toep.astype(BF16), w_st.astype(BF16), c_out.astype(BF16), lam.astype(F32)


def _s5_perm():
    gpl, N = LANES // S5_GROUP, S5_GROUP
    a = np.arange(gpl * LANES)
    tau, g, n = a // LANES, (a % LANES) // N, a % N
    p = np.zeros((gpl * LANES, gpl * LANES), np.float32)
    p[a, g * LANES + tau * N + n] = 1.0
    return p


def _s5_kernel(u_ref, perm_ref, toep_ref, wst_ref, cout_ref, lam_ref, y_ref, *, nsteps):
    L, N = S5_BLOCK, S5_GROUP
    nblk = u_ref.shape[0] // L
    gpl = LANES // N
    nchunk = L // gpl
    perm = perm_ref[...]
    v = [u_ref[pl.ds(t, nblk, stride=L), :].astype(BF16) for t in range(L)]
    uc = [jnp.dot(jnp.concatenate(v[j * gpl:(j + 1) * gpl], axis=1), perm,
                  preferred_element_type=F32).astype(BF16) for j in range(nchunk)]
    blk = lax.broadcasted_iota(jnp.int32, (nblk, wst_ref.shape[-1]), 0)
    ys = []
    for g in range(gpl):
        ug = jnp.concatenate([uc[j][:, g * LANES:(g + 1) * LANES] for j in range(nchunk)], axis=1)
        z = jnp.dot(ug, wst_ref[g], preferred_element_type=F32)
        p2 = z.shape[1]
        h = z
        for k in range(nsteps):
            sh = 1 << k
            prev = jnp.where(blk >= sh, pltpu.roll(h, sh, axis=0), 0.0)
            lam = lam_ref[g, k]
            h = h + prev * lam[0:1, :] + pltpu.roll(prev, p2 // 2, axis=1) * lam[1:2, :]
        h0 = jnp.where(blk >= 1, pltpu.roll(h, 1, axis=0), 0.0)
        yg = (jnp.dot(ug, toep_ref[g], preferred_element_type=F32)
              + jnp.dot(h0.astype(BF16), cout_ref[g], preferred_element_type=F32))
        ys.append(yg.astype(BF16))
    for j in range(nchunk):
        yj = jnp.concatenate([ys[g][:, j * LANES:(j + 1) * LANES] for g in range(gpl)], axis=1)
        wj = jnp.dot(yj, perm, preferred_element_type=F32)
        for tau in range(gpl):
            y_ref[pl.ds(j * gpl + tau, nblk, stride=L), :] = wj[:, tau * LANES:(tau + 1) * LANES]


def _s5_core(u, toep, w_st, c_out, lam, batch, seq):
    t, d = u.shape
    gpl = LANES // S5_GROUP
    ln = toep.shape[-1]
    p2 = w_st.shape[-1]
    nsteps = lam.shape[1]
    perm = jnp.asarray(_s5_perm(), BF16)
    return pl.pallas_call(
        functools.partial(_s5_kernel, nsteps=nsteps),
        grid=(d // LANES, batch),
        in_specs=[
            pl.BlockSpec((seq, LANES), lambda i, b: (b, i)),
            pl.BlockSpec(perm.shape, lambda i, b: (0, 0)),
            pl.BlockSpec((gpl, ln, ln), lambda i, b: (i, 0, 0)),
            pl.BlockSpec((gpl, ln, p2), lambda i, b: (i, 0, 0)),
            pl.BlockSpec((gpl, p2, ln), lambda i, b: (i, 0, 0)),
            pl.BlockSpec((gpl, nsteps, 2, p2), lambda i, b: (i, 0, 0, 0)),
        ],
        out_specs=pl.BlockSpec((seq, LANES), lambda i, b: (b, i)),
        out_shape=jax.ShapeDtypeStruct((t, d), F32),
        compiler_params=_params("parallel", "parallel"),
        name="s5_ssm",
    )(u, perm, toep, w_st, c_out, lam)


def _s5(x, ln, w_in, a_re, a_im, b_re, b_im, c_re, c_im, d_skip, log_dt, w_out, batch, seq):
    nblk = seq // S5_BLOCK
    u = _pro_matmul(x, ln, w_in.astype(BF16), prologue="rms", out_dtype=F32, name="s5_in_proj")
    toep, w_st, c_out, lam = _s5_tables(a_re, a_im, b_re, b_im, c_re, c_im, d_skip, log_dt, nblk)
    y = _s5_core(u, toep, w_st, c_out, lam, batch, seq)
    return _s5_out(y, w_out.astype(BF16), x)


def _router_kernel(x_ref, g_ref, w_ref, h_ref, r_ref):
    h = _rms(x_ref[...], g_ref[...])
    h_ref[...] = h
    h0 = h.astype(BF16)
    h1 = (h - h0.astype(F32)).astype(BF16)
    w = w_ref[...]
    w0 = w.astype(BF16)
    w1 = (w - w0.astype(F32)).astype(BF16)
    logits = (jnp.dot(h0, w0, preferred_element_type=F32) + jnp.dot(h0, w1, preferred_element_type=F32)
              + jnp.dot(h1, w0, preferred_element_type=F32))
    lane = lax.broadcasted_iota(jnp.int32, logits.shape, 1).astype(F32)
    neg = -jnp.inf
    lg = jnp.where(lane < N_EXPERTS, logits, neg)
    m1 = jnp.max(lg, axis=-1, keepdims=True)
    i1 = jnp.min(jnp.where(lg == m1, lane, float(LANES)), axis=-1, keepdims=True)
    lg2 = jnp.where(lane == i1, neg, lg)
    m2 = jnp.max(lg2, axis=-1, keepdims=True)
    i2 = jnp.min(jnp.where(lg2 == m2, lane, float(LANES)), axis=-1, keepdims=True)
    e2 = jnp.exp(m2 - m1)
    g1 = 1.0 / (1.0 + e2)
    g2 = e2 / (1.0 + e2)
    out = jnp.where(lane == 0, g1, jnp.where(lane == 1, g2, jnp.where(lane == 2, i1, jnp.where(lane == 3, i2, 0.0))))
    r_ref[...] = out[:, :r_ref.shape[1]]


def _router(x, gain, w_router):
    m, d = x.shape
    tm = _tile(m, TM)
    w_pad = jnp.zeros((d, LANES), F32).at[:, :N_EXPERTS].set(w_router.astype(F32))
    return pl.pallas_call(
        _router_kernel,
        grid=(m // tm,),
        in_specs=[
            pl.BlockSpec((tm, d), lambda i: (i, 0)),
            pl.BlockSpec((1, d), lambda i: (0, 0)),
            pl.BlockSpec((d, LANES), lambda i: (0, 0)),
        ],
        out_specs=[pl.BlockSpec((tm, d), lambda i: (i, 0)), pl.BlockSpec((tm, 8), lambda i: (i, 0))],
        out_shape=[jax.ShapeDtypeStruct((m, d), F32), jax.ShapeDtypeStruct((m, 8), F32)],
        compiler_params=_params("parallel"),
        name="moe_router_top2",
    )(x, gain.reshape(1, d).astype(F32), w_pad)


def _moe_kernel(te_ref, ta_ref, tv_ref, idx_ref, h_hbm, wg_ref, wu_ref, wd_ref, y_hbm,
                x_ref, xb_ref, acc_ref, yb_ref, pend_ref, sem_g, sem_s):
    i = pl.program_id(0)
    f = pl.program_id(1)
    nt = pl.num_programs(0)
    last_f = pl.num_programs(1) - 1
    tm = xb_ref.shape[0]
    slot = lax.rem(i, 2)
    active = ta_ref[i] == 1
    nxt = jnp.minimum(i + 1, nt - 1)
    carry = active & (i + 1 < nt) & (ta_ref[nxt] == 1)
    n_dma_steps = tm // GATHER_CHUNK

    def start_gather(tile, sl):
        def body(o, c):
            base = pl.multiple_of(o * DMA_UNROLL, DMA_UNROLL)
            for j in range(DMA_UNROLL):
                r = base + j
                pltpu.make_async_copy(h_hbm.at[pl.ds(idx_ref[tile, r], 1)], x_ref.at[sl, pl.ds(r, 1)],
                                      sem_g.at[sl]).start()
            return c

        lax.fori_loop(0, tm // DMA_UNROLL, body, 0)

    def wait_gather(sl):
        pltpu.make_async_copy(h_hbm.at[pl.ds(0, tm)], x_ref.at[sl], sem_g.at[sl]).wait()

    def scatter_row(r):
        pltpu.make_async_copy(yb_ref.at[pl.ds(r, 1)], y_hbm.at[pl.ds(idx_ref[i, tm + r], 1)], sem_s).start()

    def start_scatter():
        @pl.when(tv_ref[i] == tm)
        def _():
            def body(o, c):
                base = pl.multiple_of(o * DMA_UNROLL, DMA_UNROLL)
                for j in range(DMA_UNROLL):
                    scatter_row(base + j)
                return c

            lax.fori_loop(0, tm // DMA_UNROLL, body, 0)

        @pl.when(tv_ref[i] < tm)
        def _():
            def body(r, c):
                scatter_row(r)
                return c

            lax.fori_loop(0, tv_ref[i], body, 0)

    def wait_scatter():
        @pl.when(pend_ref[0] > 0)
        def _():
            nv = tv_ref[jnp.maximum(pend_ref[0] - 1, 0)]
            for bit in reversed(range(tm.bit_length())):
                n = 1 << bit

                @pl.when((nv & n) != 0)
                def _():
                    pltpu.make_async_copy(yb_ref.at[pl.ds(0, n)], y_hbm.at[pl.ds(0, n)], sem_s).wait()

            pend_ref[0] = 0

    @pl.when((i == 0) & (f == 0))
    def _():
        pend_ref[0] = 0

        @pl.when(active)
        def _():
            start_gather(0, 0)

    @pl.when(active & (f == 0))
    def _():
        wait_gather(slot)
        xb_ref[...] = x_ref[slot].astype(BF16)
        acc_ref[...] = jnp.zeros_like(acc_ref)

    @pl.when(carry & (f < n_dma_steps))
    def _():
        base = pl.multiple_of(f * GATHER_CHUNK, GATHER_CHUNK)
        for j in range(GATHER_CHUNK):
            r = base + j
            pltpu.make_async_copy(h_hbm.at[pl.ds(idx_ref[nxt, r], 1)], x_ref.at[1 - slot, pl.ds(r, 1)],
                                  sem_g.at[1 - slot]).start()
        acc_ref[...] += _swiglu_step(xb_ref[...], wg_ref[...], wu_ref[...], wd_ref[...])

    @pl.when(active & jnp.logical_not(carry & (f < n_dma_steps)))
    def _():
        acc_ref[...] += _swiglu_step(xb_ref[...], wg_ref[...], wu_ref[...], wd_ref[...])

    @pl.when(active & (f == last_f))
    def _():
        wait_scatter()
        yb_ref[...] = acc_ref[...]
        start_scatter()
        pend_ref[0] = i + 1

    @pl.when((i == nt - 1) & (f == last_f))
    def _():
        wait_scatter()


def _moe_experts(h, idx, tile_expert, tile_active, tile_valid, w_gu, w_down, layer):
    n_tok, d = h.shape
    n_tiles, tm = idx.shape[0], idx.shape[1] // 2
    ff = w_down.shape[2]
    tf = _tile(ff, TF, LANES)
    nf = ff // tf
    assert tm % GATHER_CHUNK == 0 and tm // GATHER_CHUNK <= nf and tm % DMA_UNROLL == 0

    def fe(f, i, ta):
        return jnp.where(ta[i] == 1, f, nf - 1)

    grid_spec = pltpu.PrefetchScalarGridSpec(
        num_scalar_prefetch=4,
        grid=(n_tiles, nf),
        in_specs=[
            pl.BlockSpec(memory_space=pl.ANY),
            pl.BlockSpec((None, None, d, tf), lambda i, f, te, ta, tv, ix: (layer, te[i], 0, fe(f, i, ta))),
            pl.BlockSpec((None, None, d, tf), lambda i, f, te, ta, tv, ix: (layer, te[i], 0, fe(f, i, ta) + nf)),
            pl.BlockSpec((None, None, tf, d), lambda i, f, te, ta, tv, ix: (layer, te[i], fe(f, i, ta), 0)),
        ],
        out_specs=pl.BlockSpec(memory_space=pl.ANY),
        scratch_shapes=[
            pltpu.VMEM((2, tm, d), F32),
            pltpu.VMEM((tm, d), BF16),
            pltpu.VMEM((tm, d), F32),
            pltpu.VMEM((tm, d), F32),
            pltpu.SMEM((1,), jnp.int32),
            pltpu.SemaphoreType.DMA((2,)),
            pltpu.SemaphoreType.DMA,
        ],
    )
    return pl.pallas_call(
        _moe_kernel,
        grid_spec=grid_spec,
        out_shape=jax.ShapeDtypeStruct((TOP_K * n_tok, d), F32),
        compiler_params=_params("arbitrary", "arbitrary"),
        name="moe_grouped_swiglu",
    )(tile_expert, tile_active, tile_valid, idx, h, w_gu, w_gu, w_down)


def _combine_kernel(x_ref, a_ref, b_ref, r_ref, g_ref, o_ref, *, out_norm):
    r = r_ref[...]
    out = x_ref[...] + (r[:, 0:1] * a_ref[...] + r[:, 1:2] * b_ref[...])
    o_ref[...] = _rms(out, g_ref[...]) if out_norm else out


def _combine(x, y, route, out_gain=None):
    m, d = x.shape
    tm = _tile(m, TM)
    nb = m // tm
    gain = jnp.ones((d,), F32) if out_gain is None else out_gain
    return pl.pallas_call(
        functools.partial(_combine_kernel, out_norm=out_gain is not None),
        grid=(nb,),
        in_specs=[
            pl.BlockSpec((tm, d), lambda i: (i, 0)),
            pl.BlockSpec((tm, d), lambda i: (i, 0)),
            pl.BlockSpec((tm, d), lambda i: (i + nb, 0)),
            pl.BlockSpec((tm, route.shape[1]), lambda i: (i, 0)),
            pl.BlockSpec((1, d), lambda i: (0, 0)),
        ],
        out_specs=pl.BlockSpec((tm, d), lambda i: (i, 0)),
        out_shape=jax.ShapeDtypeStruct((m, d), F32),
        compiler_params=_params("parallel"),
        name="moe_combine",
    )(x, y, y, route, gain.reshape(1, d).astype(F32))


def _route_metadata(route, n_tok, tm):
    experts = route[:, TOP_K:2 * TOP_K].T.reshape(-1).astype(jnp.int32)
    n_pairs = TOP_K * n_tok
    onehot = (experts[:, None] == jnp.arange(N_EXPERTS)[None, :]).astype(jnp.int32)
    rank = jnp.take_along_axis(jnp.cumsum(onehot, axis=0), experts[:, None], axis=1)[:, 0] - 1
    counts = jnp.sum(onehot, axis=0)
    padded = ((counts + tm - 1) // tm) * tm
    ends = jnp.cumsum(padded)
    starts = ends - padded
    rows = starts[experts] + rank
    n_rows = n_pairs + N_EXPERTS * tm
    n_rows = (n_rows // tm) * tm
    dest = jnp.full((n_rows,), -1, jnp.int32).at[rows].set(jnp.arange(n_pairs, dtype=jnp.int32))
    src = jnp.where(dest < 0, 0, dest % n_tok)
    idx = jnp.concatenate([src.reshape(-1, tm), jnp.maximum(dest, 0).reshape(-1, tm)], axis=1)
    tile_start = jnp.arange(n_rows // tm, dtype=jnp.int32) * tm
    tile_expert = jnp.sum((tile_start[:, None] >= ends[None, :]).astype(jnp.int32), axis=1)
    tile_expert = jnp.minimum(tile_expert, N_EXPERTS - 1)
    tile_active = (tile_start < ends[-1]).astype(jnp.int32)
    tile_valid = jnp.clip((starts + counts)[tile_expert] - tile_start, 0, tm).astype(jnp.int32) * tile_active
    return idx, tile_expert, tile_active, tile_valid


def _moe(x, gain, w_router, w_gu, w_down, layer, out_gain=None):
    n_tok = x.shape[0]
    h, route = _router(x, gain, w_router)
    tm = min(MOE_TM, n_tok)
    idx, tile_expert, tile_active, tile_valid = _route_metadata(route, n_tok, tm)
    y = _moe_experts(h, idx, tile_expert, tile_active, tile_valid, w_gu, w_down, layer)
    return _combine(x, y, route, out_gain)


def _final_norm_kernel(x_ref, g_ref, o_ref):
    o_ref[...] = _rms(x_ref[...], g_ref[...])


def _final_norm(x, gain):
    m, d = x.shape
    tm = _tile(m, TM)
    return pl.pallas_call(
        _final_norm_kernel,
        grid=(m // tm,),
        in_specs=[pl.BlockSpec((tm, d), lambda i: (i, 0)), pl.BlockSpec((1, d), lambda i: (0, 0))],
        out_specs=pl.BlockSpec((tm, d), lambda i: (i, 0)),
        out_shape=jax.ShapeDtypeStruct((m, d), F32),
        compiler_params=_params("parallel"),
        name="final_rmsnorm",
    )(x, gain.reshape(1, d).astype(F32))


def kernel(x, ln_mix, ln_ffn, ln_final, mla_w_in, mla_q_norm, mla_kv_norm, mla_w_uq, mla_w_ukv, mla_w_o, hg_w_in, hg_lower_bound, hg_g_norm, hg_w_o, s5_w_in, s5_a_re, s5_a_im, s5_b_re, s5_b_im, s5_c_re, s5_c_im, s5_d, s5_log_dt, s5_w_out, ffn_w_gu, ffn_w_down, moe_w_router, moe_w_gu, moe_w_down):
    batch, seq, d = x.shape
    depth = ln_mix.shape[0]
    lb_w = jax.nn.softmax(hg_lower_bound.astype(F32), axis=0)
    lower_bounds = jnp.cumsum(lb_w, axis=0) - lb_w[0]
    xt = x.reshape(batch * seq, d).astype(F32)
    ffn_gu, ffn_down = ffn_w_gu.astype(BF16), ffn_w_down.astype(BF16)
    moe_gu, moe_down = moe_w_gu.astype(BF16), moe_w_down.astype(BF16)
    for i in range(depth):
        m, j = i % N_MIXERS, i // N_MIXERS
        if m == 0:
            xt = _mla(xt, ln_mix[i], mla_w_in[j], mla_q_norm[j], mla_kv_norm[j], mla_w_uq[j], mla_w_ukv[j],
                      mla_w_o[j], batch, seq)
        elif m == 1:
            xt = _hgrn2(xt, ln_mix[i], hg_w_in[j], lower_bounds[i], hg_g_norm[j], hg_w_o[j], batch, seq)
        else:
            xt = _s5(xt, ln_mix[i], s5_w_in[j], s5_a_re[j], s5_a_im[j], s5_b_re[j], s5_b_im[j], s5_c_re[j],
                     s5_c_im[j], s5_d[j], s5_log_dt[j], s5_w_out[j], batch, seq)
        f = i // 2
        if i % 2 == 0:
            xt = _ffn(xt, ln_ffn[i], ffn_gu, ffn_down, f)
        else:
            xt = _moe(xt, ln_ffn[i], moe_w_router[f], moe_gu, moe_down, f,
                      out_gain=ln_final if i == depth - 1 else None)
    if depth % 2 == 1 or depth == 0:
        xt = _final_norm(xt, ln_final)
    return xt.reshape(batch, seq, d)
```

```python
import functools
import math

import numpy as np
import jax
import jax.numpy as jnp
from jax import lax
from jax.experimental import pallas as pl
from jax.experimental.pallas import tpu as pltpu

F32 = jnp.float32
BF16 = jnp.bfloat16
EPS = 1e-6
N_MIXERS = 3

MLA_HEADS = 16
MLA_Q_RANK = 512
MLA_KV_RANK = 512
MLA_NOPE = 128
MLA_ROPE = 64
MLA_V = 128
ROPE_THETA = 10000.0
HG_HEADS = 16
HG_CHUNK = 64
S5_GROUP = 16
S5_STATE = 64
S5_BLOCK = 32
N_EXPERTS = 8
TOP_K = 2

LANES = 128
VMEM_LIMIT = 56 * 1024 * 1024
TILE_VMEM_BUDGET = 40 * 1024 * 1024
RESIDENT_WEIGHT_BYTES = 8 * 1024 * 1024

TM = 512
TN = 1024
TF = 512
ATT_TILE = 1024
ATT_GROUPS = 4
ATT_HEADS = 8
HG_TILE = 1024
MOE_TM = 512
DMA_UNROLL = 8
GATHER_CHUNK = 64


def _tile(n, t, step=8):
    if n % step:
        return n
    best = step
    for c in range(step, min(n, t) + 1, step):
        if n % c == 0:
            best = c
    return best


def _matmul_tiles(m, k, n, vmem_bytes):
    tn = n if 2 * k * n <= RESIDENT_WEIGHT_BYTES else _tile(n, TN, LANES)
    for tm in (_tile(m, 2 * TM), _tile(m, TM)):
        if vmem_bytes(tm, tn) <= TILE_VMEM_BUDGET:
            break
    return tm, tn


def _params(*sem):
    return pltpu.CompilerParams(dimension_semantics=sem, vmem_limit_bytes=VMEM_LIMIT)


def _rms(x, g):
    return x * lax.rsqrt(jnp.mean(x * x, axis=-1, keepdims=True) + EPS) * g


def _gelu_tanh(y):
    c = math.sqrt(2.0 / math.pi)
    return 0.5 * y * (1.0 + jnp.tanh(c * (y + 0.044715 * (y * y * y))))


def _sigmoid(x):
    return 1.0 / (1.0 + jnp.exp(-x))


def _pro_matmul_kernel(x_ref, g_ref, w_ref, o_ref, h_ref, *, prologue):
    @pl.when(pl.program_id(1) == 0)
    def _():
        x = x_ref[...].astype(F32)
        if prologue == "rms":
            x = _rms(x, g_ref[...])
        h_ref[...] = x.astype(BF16)

    o_ref[...] = jnp.dot(h_ref[...], w_ref[...], preferred_element_type=F32).astype(o_ref.dtype)


def _pro_matmul(x, gain, w, *, prologue, out_dtype, x_col_block=0, name):
    m = x.shape[0]
    k, n = w.shape
    xb, ob = x.dtype.itemsize, jnp.dtype(out_dtype).itemsize
    tm, tn = _matmul_tiles(m, k, n, lambda tm, tn: 2 * tm * k * xb + 2 * tm * k + 4 * k * tn + 2 * tm * tn * ob)
    return pl.pallas_call(
        functools.partial(_pro_matmul_kernel, prologue=prologue),
        grid=(m // tm, n // tn),
        in_specs=[
            pl.BlockSpec((tm, k), lambda i, j: (i, x_col_block)),
            pl.BlockSpec((1, k), lambda i, j: (0, 0)),
            pl.BlockSpec((k, tn), lambda i, j: (0, j)),
        ],
        out_specs=pl.BlockSpec((tm, tn), lambda i, j: (i, j)),
        out_shape=jax.ShapeDtypeStruct((m, n), out_dtype),
        scratch_shapes=[pltpu.VMEM((tm, k), BF16)],
        compiler_params=_params("parallel", "arbitrary"),
        name=name,
    )(x, gain.reshape(1, k).astype(F32), w)


def _matmul_res_kernel(a_ref, w_ref, r_ref, o_ref):
    o_ref[...] = r_ref[...] + jnp.dot(a_ref[...], w_ref[...], preferred_element_type=F32)


def _matmul_res(a, w, res, *, name):
    m, k = a.shape
    n = w.shape[1]
    tm, tn = _matmul_tiles(m, k, n, lambda tm, tn: 4 * tm * k + 4 * k * tn + 16 * tm * tn)
    return pl.pallas_call(
        _matmul_res_kernel,
        grid=(m // tm, n // tn),
        in_specs=[
            pl.BlockSpec((tm, k), lambda i, j: (i, 0)),
            pl.BlockSpec((k, tn), lambda i, j: (0, j)),
            pl.BlockSpec((tm, tn), lambda i, j: (i, j)),
        ],
        out_specs=pl.BlockSpec((tm, tn), lambda i, j: (i, j)),
        out_shape=jax.ShapeDtypeStruct((m, n), F32),
        compiler_params=_params("parallel", "parallel"),
        name=name,
    )(a, w, res)


def _s5_out_kernel(y_ref, wv_ref, wg_ref, r_ref, o_ref, h_ref):
    @pl.when(pl.program_id(1) == 0)
    def _():
        h_ref[...] = _gelu_tanh(y_ref[...].astype(F32)).astype(BF16)

    h = h_ref[...]
    val = jnp.dot(h, wv_ref[...], preferred_element_type=F32)
    gate = jnp.dot(h, wg_ref[...], preferred_element_type=F32)
    o_ref[...] = r_ref[...] + val * _sigmoid(gate)


def _s5_out(y, w_out, res):
    m, k = y.shape
    n = w_out.shape[1] // 2
    tm, tn = _tile(m, 2 * TM), _tile(n, TN // 2, LANES)
    nb = n // tn
    return pl.pallas_call(
        _s5_out_kernel,
        grid=(m // tm, nb),
        in_specs=[
            pl.BlockSpec((tm, k), lambda i, j: (i, 0)),
            pl.BlockSpec((k, tn), lambda i, j: (0, j)),
            pl.BlockSpec((k, tn), lambda i, j: (0, j + nb)),
            pl.BlockSpec((tm, tn), lambda i, j: (i, j)),
        ],
        out_specs=pl.BlockSpec((tm, tn), lambda i, j: (i, j)),
        out_shape=jax.ShapeDtypeStruct((m, n), F32),
        scratch_shapes=[pltpu.VMEM((tm, k), BF16)],
        compiler_params=_params("parallel", "arbitrary"),
        name="s5_out_glu",
    )(y, w_out, w_out, res)


def _swiglu_step(h, wg, wu, wd):
    g = jnp.dot(h, wg, preferred_element_type=F32)
    u = jnp.dot(h, wu, preferred_element_type=F32)
    a = (g * _sigmoid(g) * u).astype(BF16)
    return jnp.dot(a, wd, preferred_element_type=F32)


def _ffn_kernel(x_ref, g_ref, wg_ref, wu_ref, wd_ref, o_ref, h_ref, acc_ref):
    f = pl.program_id(1)

    @pl.when(f == 0)
    def _():
        h_ref[...] = _rms(x_ref[...], g_ref[...]).astype(BF16)
        acc_ref[...] = jnp.zeros_like(acc_ref)

    acc_ref[...] += _swiglu_step(h_ref[...], wg_ref[...], wu_ref[...], wd_ref[...])

    @pl.when(f == pl.num_programs(1) - 1)
    def _():
        o_ref[...] = x_ref[...] + acc_ref[...]


def _ffn(x, gain, w_gu, w_down, layer):
    m, d = x.shape
    ff = w_down.shape[1]
    tm, tf = _tile(m, TM), _tile(ff, TF, LANES)
    nf = ff // tf
    return pl.pallas_call(
        _ffn_kernel,
        grid=(m // tm, nf),
        in_specs=[
            pl.BlockSpec((tm, d), lambda i, f: (i, 0)),
            pl.BlockSpec((1, d), lambda i, f: (0, 0)),
            pl.BlockSpec((None, d, tf), lambda i, f: (layer, 0, f)),
            pl.BlockSpec((None, d, tf), lambda i, f: (layer, 0, f + nf)),
            pl.BlockSpec((None, tf, d), lambda i, f: (layer, f, 0)),
        ],
        out_specs=pl.BlockSpec((tm, d), lambda i, f: (i, 0)),
        out_shape=jax.ShapeDtypeStruct((m, d), F32),
        scratch_shapes=[pltpu.VMEM((tm, d), BF16), pltpu.VMEM((tm, d), F32)],
        compiler_params=_params("parallel", "arbitrary"),
        name="ffn_swiglu",
    )(x, gain.reshape(1, d).astype(F32), w_gu, w_gu, w_down)


def _rope_table(seq):
    half = MLA_ROPE // 2
    inv = ROPE_THETA ** (-jnp.arange(half, dtype=F32) / half)
    ang = jnp.arange(seq, dtype=F32)[:, None] * inv
    cos, sin = jnp.cos(ang), jnp.sin(ang)
    return jnp.concatenate([cos, cos, sin, sin], axis=-1)


def _rot_cols(w):
    half = w.shape[-1] // 2
    return jnp.concatenate([-w[..., half:], w[..., :half]], axis=-1)


def _rope_slab(slab, cs):
    prod = slab * cs
    return prod + pltpu.roll(prod, MLA_ROPE, axis=1)


def _krope_kernel(p_ref, cs_ref, o_ref):
    r = _rope_slab(p_ref[...].astype(F32), cs_ref[...])
    lane = lax.broadcasted_iota(jnp.int32, r.shape, 1)
    o_ref[...] = jnp.where(lane < MLA_ROPE, r, 0.0).astype(o_ref.dtype)


def _krope(proj, cs, seq, col_block):
    t = proj.shape[0]
    tm = _tile(seq, TM)
    ns = seq // tm
    return pl.pallas_call(
        _krope_kernel,
        grid=(t // tm,),
        in_specs=[
            pl.BlockSpec((tm, LANES), lambda i: (i, col_block)),
            pl.BlockSpec((tm, LANES), lambda i: (i % ns, 0)),
        ],
        out_specs=pl.BlockSpec((tm, LANES), lambda i: (i, 0)),
        out_shape=jax.ShapeDtypeStruct((t, LANES), BF16),
        compiler_params=_params("parallel"),
        name="mla_k_rope",
    )(proj, cs)


def _attn_kernel(qi_ref, ki_ref, q_ref, cs_ref, kv_ref, kr_ref, o_ref, qc_ref, m_ref, acc_ref):
    n_heads = qc_ref.shape[0]
    s_id = pl.program_id(2)
    qi = qi_ref[s_id]
    ki = ki_ref[s_id]
    hw = 2 * LANES

    @pl.when(ki == 0)
    def _():
        for hh in range(n_heads):
            q = q_ref[:, hh * hw:(hh + 1) * hw]
            qr = _rope_slab(q[:, LANES:].astype(F32), cs_ref[...]).astype(BF16)
            qc_ref[hh] = jnp.concatenate([q[:, :LANES], qr], axis=1)
        m_ref[...] = jnp.full_like(m_ref, -jnp.inf)
        acc_ref[...] = jnp.zeros_like(acc_ref)

    lane = lax.broadcasted_iota(jnp.int32, kr_ref.shape, 1)
    ones_col = jnp.where(lane == 0, 1.0, 0.0).astype(BF16)
    rq = q_ref.shape[0] // ATT_GROUPS

    def sweep(masked):
        for hh in range(n_heads):
            kc = jnp.concatenate([kv_ref[:, hh * hw:hh * hw + LANES], kr_ref[...]], axis=1)
            ve = jnp.concatenate([kv_ref[:, hh * hw + LANES:(hh + 1) * hw], ones_col], axis=1)

            def scores(g):
                return lax.dot_general(qc_ref[hh, g * rq:(g + 1) * rq, :], kc, (((1,), (1,)), ((), ())),
                                       preferred_element_type=F32)

            def update(g, sc):
                rows = slice(g * rq, (g + 1) * rq)
                if masked:
                    row = lax.broadcasted_iota(jnp.int32, sc.shape, 0) + g * rq
                    col = lax.broadcasted_iota(jnp.int32, sc.shape, 1)
                    sc = jnp.where(col <= row, sc, -jnp.inf)
                m_prev = m_ref[hh, rows, :]
                m_new = jnp.maximum(m_prev, jnp.max(sc, axis=-1, keepdims=True))
                alpha = jnp.exp2(m_prev - m_new)
                p = jnp.exp2(sc - m_new)
                acc_ref[hh, rows, :] = alpha * acc_ref[hh, rows, :] + jnp.dot(p.astype(BF16), ve,
                                                                              preferred_element_type=F32)
                m_ref[hh, rows, :] = m_new

            nxt = scores(0)
            for g in range(ATT_GROUPS):
                cur = nxt
                if g + 1 < ATT_GROUPS:
                    nxt = scores(g + 1)
                update(g, cur)

    @pl.when(ki < qi)
    def _():
        sweep(False)

    @pl.when(ki == qi)
    def _():
        sweep(True)
        for hh in range(n_heads):
            a = acc_ref[hh]
            o_ref[:, hh * LANES:(hh + 1) * LANES] = (a[:, :LANES] / a[:, LANES:LANES + 1]).astype(o_ref.dtype)


def _attention(q_ext, kv_ext, k_rope, cs, batch, seq):
    t = q_ext.shape[0]
    heads = q_ext.shape[1] // (2 * LANES)
    hps = math.gcd(heads, ATT_HEADS)
    hw = hps * 2 * LANES
    tq = _tile(seq, ATT_TILE)
    nq = seq // tq
    steps = [(i, j) for i in range(nq) for j in range(i + 1)]
    qi_tab = jnp.asarray([s[0] for s in steps], jnp.int32)
    ki_tab = jnp.asarray([s[1] for s in steps], jnp.int32)
    grid_spec = pltpu.PrefetchScalarGridSpec(
        num_scalar_prefetch=2,
        grid=(batch, heads // hps, len(steps)),
        in_specs=[
            pl.BlockSpec((tq, hw), lambda b, h, s, qi, ki: (b * nq + qi[s], h)),
            pl.BlockSpec((tq, LANES), lambda b, h, s, qi, ki: (qi[s], 0)),
            pl.BlockSpec((tq, hw), lambda b, h, s, qi, ki: (b * nq + ki[s], h)),
            pl.BlockSpec((tq, LANES), lambda b, h, s, qi, ki: (b * nq + ki[s], 0)),
        ],
        out_specs=pl.BlockSpec((tq, hps * LANES), lambda b, h, s, qi, ki: (b * nq + qi[s], h)),
        scratch_shapes=[
            pltpu.VMEM((hps, tq, 2 * LANES), BF16),
            pltpu.VMEM((hps, tq, 1), F32),
            pltpu.VMEM((hps, tq, 2 * LANES), F32),
        ],
    )
    return pl.pallas_call(
        _attn_kernel,
        grid_spec=grid_spec,
        out_shape=jax.ShapeDtypeStruct((t, heads * LANES), BF16),
        compiler_params=_params("parallel", "parallel", "arbitrary"),
        name="mla_flash_attention",
    )(qi_tab, ki_tab, q_ext, cs, kv_ext, k_rope)


def _mla(x, ln, w_in, q_norm, kv_norm, w_uq, w_ukv, w_o, batch, seq):
    heads = MLA_HEADS
    assert MLA_NOPE == LANES and MLA_V == LANES and 2 * MLA_ROPE == LANES
    assert MLA_Q_RANK == MLA_KV_RANK
    rank = MLA_Q_RANK
    w_kr = w_in[:, 2 * rank:]
    w_in_ext = jnp.concatenate([w_in, _rot_cols(w_kr)], axis=1).astype(BF16)
    proj = _pro_matmul(x, ln, w_in_ext, prologue="rms", out_dtype=F32, name="mla_in_proj")
    scale = (MLA_NOPE + MLA_ROPE) ** -0.5 * math.log2(math.e)
    wq = w_uq.reshape(rank, heads, MLA_NOPE + MLA_ROPE) * scale
    wq_ext = jnp.concatenate([wq, _rot_cols(wq[..., MLA_NOPE:])], axis=-1)
    wq_ext = wq_ext.reshape(rank, heads * 2 * LANES).astype(BF16)
    q_ext = _pro_matmul(proj, q_norm, wq_ext, prologue="rms", out_dtype=BF16, x_col_block=0, name="mla_q_up")
    kv_ext = _pro_matmul(proj, kv_norm, w_ukv.astype(BF16), prologue="rms", out_dtype=BF16, x_col_block=1,
                         name="mla_kv_up")
    cs = _rope_table(seq)
    k_rope = _krope(proj, cs, seq, col_block=2 * rank // LANES)
    o = _attention(q_ext, kv_ext, k_rope, cs, batch, seq)
    return _matmul_res(o, w_o.astype(BF16), x, name="mla_out_proj")


def _hg_tables(c):
    levels = int(math.log2(c))
    assert 2 ** levels == c
    t = np.arange(c)[:, None]
    u = np.arange(c)[None, :]
    mats = []
    for l in range(1, levels + 1):
        mid = ((t >> l) << l) + (1 << (l - 1))
        upper = t >= mid
        mats.append(np.where(upper, (u >= mid) & (u <= t), (u > t) & (u <= mid - 1)))
    mats.append(u <= t)
    mats.append(u > t)
    masks = [t == u]
    for l in range(1, levels + 1):
        masks.append(((t >> l) == (u >> l)) & (((t >> (l - 1)) & 1) == 1) & (((u >> (l - 1)) & 1) == 0))
    return (np.concatenate(mats, axis=0).astype(np.float32), np.concatenate(masks, axis=0).astype(np.float32),
            levels)


def _split3(x):
    hi = x.astype(BF16)
    r = x - hi.astype(F32)
    mid = r.astype(BF16)
    lo = (r - mid.astype(F32)).astype(BF16)
    return hi, mid, lo


def _hgrn2_kernel(q_ref, z_ref, v_ref, g_ref, lb_ref, gn_ref, tab_ref, msk_ref, o_ref, st_ref, *, chunk, levels):
    c = chunk
    n_chunks = q_ref.shape[0] // c
    dk = q_ref.shape[1]

    @pl.when(pl.program_id(2) == 0)
    def _():
        st_ref[...] = jnp.zeros_like(st_ref)

    lb = lb_ref[...]
    log_lb = jnp.log(lb)
    log_1mlb = jnp.log1p(-lb)
    z = z_ref[...]
    q = q_ref[...].astype(F32)
    e = jnp.exp(-jnp.abs(z))
    log_sig = jnp.minimum(z, 0.0) - jnp.log1p(e)
    bterm = log_1mlb + log_sig
    lf = jnp.maximum(log_lb, bterm) + jnp.log1p(jnp.exp(-jnp.abs(log_lb - bterm)))
    k = (1.0 - lb) * jnp.where(z >= 0, e, 1.0) / (1.0 + e)

    def chunks_on_lanes(x):
        return jnp.concatenate([x[i * c:(i + 1) * c] for i in range(n_chunks)], axis=1)

    tab = tab_ref[...]
    p0, p1, p2 = _split3(lf)
    sums = (jnp.dot(tab, chunks_on_lanes(p0), preferred_element_type=F32)
            + jnp.dot(tab, chunks_on_lanes(p1), preferred_element_type=F32)
            + jnp.dot(tab, chunks_on_lanes(p2), preferred_element_type=F32))
    dec = jnp.exp(sums)
    diag = jnp.sum(q * k, axis=-1, keepdims=True)
    o_intra, q_in, upds, d_last = [], [], [], []
    for i in range(n_chunks):
        rows = slice(i * c, (i + 1) * c)
        qi, ki, vi = q[rows], k[rows], v_ref[rows, :]
        di = dec[:, i * dk:(i + 1) * dk]
        scores = msk_ref[0:c, :] * diag[rows]
        for l in range(1, levels + 1):
            d = di[(l - 1) * c:l * c]
            sl_ = lax.dot_general((qi * d).astype(BF16), (ki * d).astype(BF16), (((1,), (1,)), ((), ())),
                                  preferred_element_type=F32)
            scores = scores + msk_ref[l * c:(l + 1) * c, :] * sl_
        d_pre = di[levels * c:(levels + 1) * c]
        d_suf = di[(levels + 1) * c:(levels + 2) * c]
        o_intra.append(jnp.dot(scores.astype(BF16), vi, preferred_element_type=F32))
        q_in.append((qi * d_pre).astype(BF16))
        upds.append(lax.dot_general(vi, (ki * d_suf).astype(BF16), (((0,), (0,)), ((), ())),
                                    preferred_element_type=F32))
        d_last.append(d_pre[c - 1:c, :])
    st = st_ref[...]
    outs = []
    for i in range(n_chunks):
        outs.append(o_intra[i] + lax.dot_general(q_in[i], st.astype(BF16), (((1,), (1,)), ((), ())),
                                                 preferred_element_type=F32))
        st = st * d_last[i] + upds[i]
    st_ref[...] = st
    o = jnp.concatenate(outs, axis=0)
    gt = g_ref[...].astype(F32)
    o_ref[...] = (_rms(o, gn_ref[...]) * (gt * _sigmoid(gt))).astype(o_ref.dtype)


def _hgrn2_core(qvg, z, lb, g_norm, batch, seq):
    t = z.shape[0]
    heads = z.shape[1] // LANES
    ts = _tile(seq, HG_TILE)
    ns = seq // ts
    c = min(HG_CHUNK, ts)
    tab, msk, levels = _hg_tables(c)
    tab = jnp.asarray(tab, BF16)
    msk = jnp.asarray(msk, F32)
    return pl.pallas_call(
        functools.partial(_hgrn2_kernel, chunk=c, levels=levels),
        grid=(batch, heads, ns),
        in_specs=[
            pl.BlockSpec((ts, LANES), lambda b, h, s: (b * ns + s, h)),
            pl.BlockSpec((ts, LANES), lambda b, h, s: (b * ns + s, h)),
            pl.BlockSpec((ts, LANES), lambda b, h, s: (b * ns + s, heads + h)),
            pl.BlockSpec((ts, LANES), lambda b, h, s: (b * ns + s, 2 * heads + h)),
            pl.BlockSpec((1, LANES), lambda b, h, s: (0, h)),
            pl.BlockSpec((1, LANES), lambda b, h, s: (0, 0)),
            pl.BlockSpec(tab.shape, lambda b, h, s: (0, 0)),
            pl.BlockSpec(msk.shape, lambda b, h, s: (0, 0)),
        ],
        out_specs=pl.BlockSpec((ts, LANES), lambda b, h, s: (b * ns + s, h)),
        out_shape=jax.ShapeDtypeStruct((t, heads * LANES), BF16),
        scratch_shapes=[pltpu.VMEM((LANES, LANES), F32)],
        compiler_params=_params("parallel", "parallel", "arbitrary"),
        name="hgrn2_recurrence",
    )(qvg, z, qvg, qvg, lb.reshape(1, -1).astype(F32), g_norm.reshape(1, -1).astype(F32), tab, msk)


def _hgrn2(x, ln, w_in, lb, g_norm, w_o, batch, seq):
    d = x.shape[1]
    assert d == HG_HEADS * LANES
    w_qvg = jnp.concatenate([w_in[:, :d], w_in[:, 2 * d:]], axis=1).astype(BF16)
    w_z = w_in[:, d:2 * d].astype(BF16)
    qvg = _pro_matmul(x, ln, w_qvg, prologue="rms", out_dtype=BF16, name="hg_in_proj_qvg")
    z = _pro_matmul(x, ln, w_z, prologue="rms", out_dtype=F32, name="hg_in_proj_z")
    o = _hgrn2_core(qvg, z, lb, g_norm, batch, seq)
    return _matmul_res(o, w_o.astype(BF16), x, name="hg_out_proj")


def _s5_tables(a_re, a_im, b_re, b_im, c_re, c_im, d_skip, log_dt, nblk):
    L = S5_BLOCK
    G, P = a_re.shape
    N = b_re.shape[-1]
    a = lax.complex(a_re.astype(F32), a_im.astype(F32))
    dt_a = a * jnp.exp(log_dt.astype(F32))[:, None]
    a_bar = jnp.exp(dt_a)
    b_bar = ((a_bar - 1.0) / a)[:, :, None] * lax.complex(b_re.astype(F32), b_im.astype(F32))
    cc = lax.complex(c_re.astype(F32), c_im.astype(F32))
    pw = jnp.exp(jnp.arange(L + 1, dtype=F32)[:, None, None] * dt_a)
    kern = jnp.real(jnp.einsum("gnp,tgp,gpm->gtnm", cc, pw[:L], b_bar))
    kern = kern.at[:, 0].add(jax.vmap(jnp.diag)(d_skip.astype(F32)))
    s_idx = jnp.arange(L)[:, None]
    t_idx = jnp.arange(L)[None, :]
    lag = t_idx - s_idx
    toep = jnp.where((lag >= 0)[None, :, :, None, None], kern[:, jnp.clip(lag, 0, L - 1)], 0.0)
    toep = toep.transpose(0, 1, 4, 2, 3).reshape(G, L * N, L * N)
    wst = jnp.einsum("sgp,gpm->gsmp", pw[:L][::-1], b_bar)
    w_st = jnp.concatenate([jnp.real(wst), jnp.imag(wst)], axis=-1).reshape(G, L * N, 2 * P)
    co = jnp.einsum("gnp,tgp->gptn", cc, pw[1:])
    c_out = jnp.concatenate([jnp.real(co), -jnp.imag(co)], axis=1).reshape(G, 2 * P, L * N)
    nsteps = max(1, int(math.ceil(math.log2(nblk))))
    lam_k = jnp.exp((L * 2.0 ** jnp.arange(nsteps, dtype=F32))[None, :, None] * dt_a[:, None, :])
    lr, li = jnp.real(lam_k), jnp.imag(lam_k)
    lam = jnp.stack([jnp.concatenate([lr, lr], -1), jnp.concatenate([-li, li], -1)], axis=2)
    return toep.astype(BF16), w_st.astype(BF16), c_out.astype(BF16), lam.astype(F32)


def _s5_perm():
    gpl, N = LANES // S5_GROUP, S5_GROUP
    a = np.arange(gpl * LANES)
    tau, g, n = a // LANES, (a % LANES) // N, a % N
    p = np.zeros((gpl * LANES, gpl * LANES), np.float32)
    p[a, g * LANES + tau * N + n] = 1.0
    return p


def _s5_kernel(u_ref, perm_ref, toep_ref, wst_ref, cout_ref, lam_ref, y_ref, *, nsteps):
    L, N = S5_BLOCK, S5_GROUP
    nblk = u_ref.shape[0] // L
    gpl = LANES // N
    nchunk = L // gpl
    perm = perm_ref[...]
    v = [u_ref[pl.ds(t, nblk, stride=L), :].astype(BF16) for t in range(L)]
    uc = [jnp.dot(jnp.concatenate(v[j * gpl:(j + 1) * gpl], axis=1), perm,
                  preferred_element_type=F32).astype(BF16) for j in range(nchunk)]
    blk = lax.broadcasted_iota(jnp.int32, (nblk, wst_ref.shape[-1]), 0)
    ys = []
    for g in range(gpl):
        ug = jnp.concatenate([uc[j][:, g * LANES:(g + 1) * LANES] for j in range(nchunk)], axis=1)
        z = jnp.dot(ug, wst_ref[g], preferred_element_type=F32)
        p2 = z.shape[1]
        h = z
        for k in range(nsteps):
            sh = 1 << k
            prev = jnp.where(blk >= sh, pltpu.roll(h, sh, axis=0), 0.0)
            lam = lam_ref[g, k]
            h = h + prev * lam[0:1, :] + pltpu.roll(prev, p2 // 2, axis=1) * lam[1:2, :]
        h0 = jnp.where(blk >= 1, pltpu.roll(h, 1, axis=0), 0.0)
        yg = (jnp.dot(ug, toep_ref[g], preferred_element_type=F32)
              + jnp.dot(h0.astype(BF16), cout_ref[g], preferred_element_type=F32))
        ys.append(yg.astype(BF16))
    for j in range(nchunk):
        yj = jnp.concatenate([ys[g][:, j * LANES:(j + 1) * LANES] for g in range(gpl)], axis=1)
        wj = jnp.dot(yj, perm, preferred_element_type=F32)
        for tau in range(gpl):
            y_ref[pl.ds(j * gpl + tau, nblk, stride=L), :] = wj[:, tau * LANES:(tau + 1) * LANES]


def _s5_core(u, toep, w_st, c_out, lam, batch, seq):
    t, d = u.shape
    gpl = LANES // S5_GROUP
    ln = toep.shape[-1]
    p2 = w_st.shape[-1]
    nsteps = lam.shape[1]
    perm = jnp.asarray(_s5_perm(), BF16)
    return pl.pallas_call(
        functools.partial(_s5_kernel, nsteps=nsteps),
        grid=(d // LANES, batch),
        in_specs=[
            pl.BlockSpec((seq, LANES), lambda i, b: (b, i)),
            pl.BlockSpec(perm.shape, lambda i, b: (0, 0)),
            pl.BlockSpec((gpl, ln, ln), lambda i, b: (i, 0, 0)),
            pl.BlockSpec((gpl, ln, p2), lambda i, b: (i, 0, 0)),
            pl.BlockSpec((gpl, p2, ln), lambda i, b: (i, 0, 0)),
            pl.BlockSpec((gpl, nsteps, 2, p2), lambda i, b: (i, 0, 0, 0)),
        ],
        out_specs=pl.BlockSpec((seq, LANES), lambda i, b: (b, i)),
        out_shape=jax.ShapeDtypeStruct((t, d), F32),
        compiler_params=_params("parallel", "parallel"),
        name="s5_ssm",
    )(u, perm, toep, w_st, c_out, lam)


def _s5(x, ln, w_in, a_re, a_im, b_re, b_im, c_re, c_im, d_skip, log_dt, w_out, batch, seq):
    nblk = seq // S5_BLOCK
    u = _pro_matmul(x, ln, w_in.astype(BF16), prologue="rms", out_dtype=F32, name="s5_in_proj")
    toep, w_st, c_out, lam = _s5_tables(a_re, a_im, b_re, b_im, c_re, c_im, d_skip, log_dt, nblk)
    y = _s5_core(u, toep, w_st, c_out, lam, batch, seq)
    return _s5_out(y, w_out.astype(BF16), x)


def _router_kernel(x_ref, g_ref, w_ref, h_ref, r_ref):
    h = _rms(x_ref[...], g_ref[...])
    h_ref[...] = h
    h0 = h.astype(BF16)
    h1 = (h - h0.astype(F32)).astype(BF16)
    w = w_ref[...]
    w0 = w.astype(BF16)
    w1 = (w - w0.astype(F32)).astype(BF16)
    logits = (jnp.dot(h0, w0, preferred_element_type=F32) + jnp.dot(h0, w1, preferred_element_type=F32)
              + jnp.dot(h1, w0, preferred_element_type=F32))
    lane = lax.broadcasted_iota(jnp.int32, logits.shape, 1).astype(F32)
    neg = -jnp.inf
    lg = jnp.where(lane < N_EXPERTS, logits, neg)
    m1 = jnp.max(lg, axis=-1, keepdims=True)
    i1 = jnp.min(jnp.where(lg == m1, lane, float(LANES)), axis=-1, keepdims=True)
    lg2 = jnp.where(lane == i1, neg, lg)
    m2 = jnp.max(lg2, axis=-1, keepdims=True)
    i2 = jnp.min(jnp.where(lg2 == m2, lane, float(LANES)), axis=-1, keepdims=True)
    e2 = jnp.exp(m2 - m1)
    g1 = 1.0 / (1.0 + e2)
    g2 = e2 / (1.0 + e2)
    out = jnp.where(lane == 0, g1, jnp.where(lane == 1, g2, jnp.where(lane == 2, i1, jnp.where(lane == 3, i2, 0.0))))
    r_ref[...] = out[:, :r_ref.shape[1]]


def _router(x, gain, w_router):
    m, d = x.shape
    tm = _tile(m, TM)
    w_pad = jnp.zeros((d, LANES), F32).at[:, :N_EXPERTS].set(w_router.astype(F32))
    return pl.pallas_call(
        _router_kernel,
        grid=(m // tm,),
        in_specs=[
            pl.BlockSpec((tm, d), lambda i: (i, 0)),
            pl.BlockSpec((1, d), lambda i: (0, 0)),
            pl.BlockSpec((d, LANES), lambda i: (0, 0)),
        ],
        out_specs=[pl.BlockSpec((tm, d), lambda i: (i, 0)), pl.BlockSpec((tm, 8), lambda i: (i, 0))],
        out_shape=[jax.ShapeDtypeStruct((m, d), F32), jax.ShapeDtypeStruct((m, 8), F32)],
        compiler_params=_params("parallel"),
        name="moe_router_top2",
    )(x, gain.reshape(1, d).astype(F32), w_pad)


def _moe_kernel(te_ref, ta_ref, tv_ref, idx_ref, h_hbm, wg_ref, wu_ref, wd_ref, y_hbm,
                x_ref, xb_ref, acc_ref, yb_ref, pend_ref, sem_g, sem_s):
    i = pl.program_id(0)
    f = pl.program_id(1)
    nt = pl.num_programs(0)
    last_f = pl.num_programs(1) - 1
    tm = xb_ref.shape[0]
    slot = lax.rem(i, 2)
    active = ta_ref[i] == 1
    nxt = jnp.minimum(i + 1, nt - 1)
    carry = active & (i + 1 < nt) & (ta_ref[nxt] == 1)
    n_dma_steps = tm // GATHER_CHUNK

    def start_gather(tile, sl):
        def body(o, c):
            base = pl.multiple_of(o * DMA_UNROLL, DMA_UNROLL)
            for j in range(DMA_UNROLL):
                r = base + j
                pltpu.make_async_copy(h_hbm.at[pl.ds(idx_ref[tile, r], 1)], x_ref.at[sl, pl.ds(r, 1)],
                                      sem_g.at[sl]).start()
            return c

        lax.fori_loop(0, tm // DMA_UNROLL, body, 0)

    def wait_gather(sl):
        pltpu.make_async_copy(h_hbm.at[pl.ds(0, tm)], x_ref.at[sl], sem_g.at[sl]).wait()

    def scatter_row(r):
        pltpu.make_async_copy(yb_ref.at[pl.ds(r, 1)], y_hbm.at[pl.ds(idx_ref[i, tm + r], 1)], sem_s).start()

    def start_scatter():
        @pl.when(tv_ref[i] == tm)
        def _():
            def body(o, c):
                base = pl.multiple_of(o * DMA_UNROLL, DMA_UNROLL)
                for j in range(DMA_UNROLL):
                    scatter_row(base + j)
                return c

            lax.fori_loop(0, tm // DMA_UNROLL, body, 0)

        @pl.when(tv_ref[i] < tm)
        def _():
            def body(r, c):
                scatter_row(r)
                return c

            lax.fori_loop(0, tv_ref[i], body, 0)

    def wait_scatter():
        @pl.when(pend_ref[0] > 0)
        def _():
            nv = tv_ref[jnp.maximum(pend_ref[0] - 1, 0)]
            for bit in reversed(range(tm.bit_length())):
                n = 1 << bit

                @pl.when((nv & n) != 0)
                def _():
                    pltpu.make_async_copy(yb_ref.at[pl.ds(0, n)], y_hbm.at[pl.ds(0, n)], sem_s).wait()

            pend_ref[0] = 0

    @pl.when((i == 0) & (f == 0))
    def _():
        pend_ref[0] = 0

        @pl.when(active)
        def _():
            start_gather(0, 0)

    @pl.when(active & (f == 0))
    def _():
        wait_gather(slot)
        xb_ref[...] = x_ref[slot].astype(BF16)
        acc_ref[...] = jnp.zeros_like(acc_ref)

    @pl.when(carry & (f < n_dma_steps))
    def _():
        base = pl.multiple_of(f * GATHER_CHUNK, GATHER_CHUNK)
        for j in range(GATHER_CHUNK):
            r = base + j
            pltpu.make_async_copy(h_hbm.at[pl.ds(idx_ref[nxt, r], 1)], x_ref.at[1 - slot, pl.ds(r, 1)],
                                  sem_g.at[1 - slot]).start()
        acc_ref[...] += _swiglu_step(xb_ref[...], wg_ref[...], wu_ref[...], wd_ref[...])

    @pl.when(active & jnp.logical_not(carry & (f < n_dma_steps)))
    def _():
        acc_ref[...] += _swiglu_step(xb_ref[...], wg_ref[...], wu_ref[...], wd_ref[...])

    @pl.when(active & (f == last_f))
    def _():
        wait_scatter()
        yb_ref[...] = acc_ref[...]
        start_scatter()
        pend_ref[0] = i + 1

    @pl.when((i == nt - 1) & (f == last_f))
    def _():
        wait_scatter()


def _moe_experts(h, idx, tile_expert, tile_active, tile_valid, w_gu, w_down, layer):
    n_tok, d = h.shape
    n_tiles, tm = idx.shape[0], idx.shape[1] // 2
    ff = w_down.shape[2]
    tf = _tile(ff, TF, LANES)
    nf = ff // tf
    assert tm % GATHER_CHUNK == 0 and tm // GATHER_CHUNK <= nf and tm % DMA_UNROLL == 0

    def fe(f, i, ta):
        return jnp.where(ta[i] == 1, f, nf - 1)

    grid_spec = pltpu.PrefetchScalarGridSpec(
        num_scalar_prefetch=4,
        grid=(n_tiles, nf),
        in_specs=[
            pl.BlockSpec(memory_space=pl.ANY),
            pl.BlockSpec((None, None, d, tf), lambda i, f, te, ta, tv, ix: (layer, te[i], 0, fe(f, i, ta))),
            pl.BlockSpec((None, None, d, tf), lambda i, f, te, ta, tv, ix: (layer, te[i], 0, fe(f, i, ta) + nf)),
            pl.BlockSpec((None, None, tf, d), lambda i, f, te, ta, tv, ix: (layer, te[i], fe(f, i, ta), 0)),
        ],
        out_specs=pl.BlockSpec(memory_space=pl.ANY),
        scratch_shapes=[
            pltpu.VMEM((2, tm, d), F32),
            pltpu.VMEM((tm, d), BF16),
            pltpu.VMEM((tm, d), F32),
            pltpu.VMEM((tm, d), F32),
            pltpu.SMEM((1,), jnp.int32),
            pltpu.SemaphoreType.DMA((2,)),
            pltpu.SemaphoreType.DMA,
        ],
    )
    return pl.pallas_call(
        _moe_kernel,
        grid_spec=grid_spec,
        out_shape=jax.ShapeDtypeStruct((TOP_K * n_tok, d), F32),
        compiler_params=_params("arbitrary", "arbitrary"),
        name="moe_grouped_swiglu",
    )(tile_expert, tile_active, tile_valid, idx, h, w_gu, w_gu, w_down)


def _combine_kernel(x_ref, a_ref, b_ref, r_ref, g_ref, o_ref, *, out_norm):
    r = r_ref[...]
    out = x_ref[...] + (r[:, 0:1] * a_ref[...] + r[:, 1:2] * b_ref[...])
    o_ref[...] = _rms(out, g_ref[...]) if out_norm else out


def _combine(x, y, route, out_gain=None):
    m, d = x.shape
    tm = _tile(m, TM)
    nb = m // tm
    gain = jnp.ones((d,), F32) if out_gain is None else out_gain
    return pl.pallas_call(
        functools.partial(_combine_kernel, out_norm=out_gain is not None),
        grid=(nb,),
        in_specs=[
            pl.BlockSpec((tm, d), lambda i: (i, 0)),
            pl.BlockSpec((tm, d), lambda i: (i, 0)),
            pl.BlockSpec((tm, d), lambda i: (i + nb, 0)),
            pl.BlockSpec((tm, route.shape[1]), lambda i: (i, 0)),
            pl.BlockSpec((1, d), lambda i: (0, 0)),
        ],
        out_specs=pl.BlockSpec((tm, d), lambda i: (i, 0)),
        out_shape=jax.ShapeDtypeStruct((m, d), F32),
        compiler_params=_params("parallel"),
        name="moe_combine",
    )(x, y, y, route, gain.reshape(1, d).astype(F32))


def _route_metadata(route, n_tok, tm):
    experts = route[:, TOP_K:2 * TOP_K].T.reshape(-1).astype(jnp.int32)
    n_pairs = TOP_K * n_tok
    onehot = (experts[:, None] == jnp.arange(N_EXPERTS)[None, :]).astype(jnp.int32)
    rank = jnp.take_along_axis(jnp.cumsum(onehot, axis=0), experts[:, None], axis=1)[:, 0] - 1
    counts = jnp.sum(onehot, axis=0)
    padded = ((counts + tm - 1) // tm) * tm
    ends = jnp.cumsum(padded)
    starts = ends - padded
    rows = starts[experts] + rank
    n_rows = n_pairs + N_EXPERTS * tm
    n_rows = (n_rows // tm) * tm
    dest = jnp.full((n_rows,), -1, jnp.int32).at[rows].set(jnp.arange(n_pairs, dtype=jnp.int32))
    src = jnp.where(dest < 0, 0, dest % n_tok)
    idx = jnp.concatenate([src.reshape(-1, tm), jnp.maximum(dest, 0).reshape(-1, tm)], axis=1)
    tile_start = jnp.arange(n_rows // tm, dtype=jnp.int32) * tm
    tile_expert = jnp.sum((tile_start[:, None] >= ends[None, :]).astype(jnp.int32), axis=1)
    tile_expert = jnp.minimum(tile_expert, N_EXPERTS - 1)
    tile_active = (tile_start < ends[-1]).astype(jnp.int32)
    tile_valid = jnp.clip((starts + counts)[tile_expert] - tile_start, 0, tm).astype(jnp.int32) * tile_active
    return idx, tile_expert, tile_active, tile_valid


def _moe(x, gain, w_router, w_gu, w_down, layer, out_gain=None):
    n_tok = x.shape[0]
    h, route = _router(x, gain, w_router)
    tm = min(MOE_TM, n_tok)
    idx, tile_expert, tile_active, tile_valid = _route_metadata(route, n_tok, tm)
    y = _moe_experts(h, idx, tile_expert, tile_active, tile_valid, w_gu, w_down, layer)
    return _combine(x, y, route, out_gain)


def _final_norm_kernel(x_ref, g_ref, o_ref):
    o_ref[...] = _rms(x_ref[...], g_ref[...])


def _final_norm(x, gain):
    m, d = x.shape
    tm = _tile(m, TM)
    return pl.pallas_call(
        _final_norm_kernel,
        grid=(m // tm,),
        in_specs=[pl.BlockSpec((tm, d), lambda i: (i, 0)), pl.BlockSpec((1, d), lambda i: (0, 0))],
        out_specs=pl.BlockSpec((tm, d), lambda i: (i, 0)),
        out_shape=jax.ShapeDtypeStruct((m, d), F32),
        compiler_params=_params("parallel"),
        name="final_rmsnorm",
    )(x, gain.reshape(1, d).astype(F32))


def kernel(x, ln_mix, ln_ffn, ln_final, mla_w_in, mla_q_norm, mla_kv_norm, mla_w_uq, mla_w_ukv, mla_w_o, hg_w_in, hg_lower_bound, hg_g_norm, hg_w_o, s5_w_in, s5_a_re, s5_a_im, s5_b_re, s5_b_im, s5_c_re, s5_c_im, s5_d, s5_log_dt, s5_w_out, ffn_w_gu, ffn_w_down, moe_w_router, moe_w_gu, moe_w_down):
    batch, seq, d = x.shape
    depth = ln_mix.shape[0]
    lb_w = jax.nn.softmax(hg_lower_bound.astype(F32), axis=0)
    lower_bounds = jnp.cumsum(lb_w, axis=0) - lb_w[0]
    xt = x.reshape(batch * seq, d).astype(F32)
    ffn_gu, ffn_down = ffn_w_gu.astype(BF16), ffn_w_down.astype(BF16)
    moe_gu, moe_down = moe_w_gu.astype(BF16), moe_w_down.astype(BF16)
    for i in range(depth):
        m, j = i % N_MIXERS, i // N_MIXERS
        if m == 0:
            xt = _mla(xt, ln_mix[i], mla_w_in[j], mla_q_norm[j], mla_kv_norm[j], mla_w_uq[j], mla_w_ukv[j],
                      mla_w_o[j], batch, seq)
        elif m == 1:
            xt = _hgrn2(xt, ln_mix[i], hg_w_in[j], lower_bounds[i], hg_g_norm[j], hg_w_o[j], batch, seq)
        else:
            xt = _s5(xt, ln_mix[i], s5_w_in[j], s5_a_re[j], s5_a_im[j], s5_b_re[j], s5_b_im[j], s5_c_re[j],
                     s5_c_im[j], s5_d[j], s5_log_dt[j], s5_w_out[j], batch, seq)
        f = i // 2
        if i % 2 == 0:
            xt = _ffn(xt, ln_ffn[i], ffn_gu, ffn_down, f)
        else:
            xt = _moe(xt, ln_ffn[i], moe_w_router[f], moe_gu, moe_down, f,
                      out_gain=ln_final if i == depth - 1 else None)
    if depth % 2 == 1 or depth == 0:
        xt = _final_norm(xt, ln_final)
    return xt.reshape(batch, seq, d)
```

```python
import functools
import math

import numpy as np
import jax
import jax.numpy as jnp
from jax import lax
from jax.experimental import pallas as pl
from jax.experimental.pallas import tpu as pltpu

F32 = jnp.float32
BF16 = jnp.bfloat16
EPS = 1e-6
N_MIXERS = 3

MLA_HEADS = 16
MLA_Q_RANK = 512
MLA_KV_RANK = 512
MLA_NOPE = 128
MLA_ROPE = 64
MLA_V = 128
ROPE_THETA = 10000.0
HG_HEADS = 16
HG_CHUNK = 64
S5_GROUP = 16
S5_BLOCK = 32
N_EXPERTS = 8
TOP_K = 2

LANES = 128
VMEM_LIMIT = 56 * 1024 * 1024
TILE_VMEM_BUDGET = 40 * 1024 * 1024
RESIDENT_WEIGHT_BYTES = 8 * 1024 * 1024

TM = 512
TN = 1024
TF = 512
ATT_TILE = 1024
ATT_GROUPS = 4
ATT_HEADS = 8
HG_TILE = 1024
MOE_TM = 512
DMA_UNROLL = 8
GATHER_CHUNK = 64


def _tile(n, t, step=8):
    if n % step:
        return n
    best = step
    for c in range(step, min(n, t) + 1, step):
        if n % c == 0:
            best = c
    return best


def _matmul_tiles(m, k, n, vmem_bytes):
    tn = n if 2 * k * n <= RESIDENT_WEIGHT_BYTES else _tile(n, TN, LANES)
    for tm in (_tile(m, 2 * TM), _tile(m, TM)):
        if vmem_bytes(tm, tn) <= TILE_VMEM_BUDGET:
            break
    return tm, tn


def _params(*sem):
    return pltpu.CompilerParams(dimension_semantics=sem, vmem_limit_bytes=VMEM_LIMIT)


def _rms(x, g):
    return x * lax.rsqrt(jnp.mean(x * x, axis=-1, keepdims=True) + EPS) * g


def _gelu_tanh(y):
    c = math.sqrt(2.0 / math.pi)
    return 0.5 * y * (1.0 + jnp.tanh(c * (y + 0.044715 * (y * y * y))))


def _sigmoid(x):
    return 1.0 / (1.0 + jnp.exp(-x))


def _pro_matmul_kernel(x_ref, g_ref, w_ref, o_ref, h_ref):
    @pl.when(pl.program_id(1) == 0)
    def _():
        h_ref[...] = _rms(x_ref[...].astype(F32), g_ref[...]).astype(BF16)

    o_ref[...] = jnp.dot(h_ref[...], w_ref[...], preferred_element_type=F32).astype(o_ref.dtype)


def _pro_matmul(x, gain, w, *, out_dtype, x_col_block=0, name):
    m = x.shape[0]
    k, n = w.shape
    xb, ob = x.dtype.itemsize, jnp.dtype(out_dtype).itemsize
    tm, tn = _matmul_tiles(m, k, n, lambda tm, tn: 2 * tm * k * xb + 2 * tm * k + 4 * k * tn + 2 * tm * tn * ob)
    return pl.pallas_call(
        _pro_matmul_kernel,
        grid=(m // tm, n // tn),
        in_specs=[
            pl.BlockSpec((tm, k), lambda i, j: (i, x_col_block)),
            pl.BlockSpec((1, k), lambda i, j: (0, 0)),
            pl.BlockSpec((k, tn), lambda i, j: (0, j)),
        ],
        out_specs=pl.BlockSpec((tm, tn), lambda i, j: (i, j)),
        out_shape=jax.ShapeDtypeStruct((m, n), out_dtype),
        scratch_shapes=[pltpu.VMEM((tm, k), BF16)],
        compiler_params=_params("parallel", "arbitrary"),
        name=name,
    )(x, gain.reshape(1, k).astype(F32), w)


def _matmul_res_kernel(a_ref, w_ref, r_ref, o_ref):
    o_ref[...] = r_ref[...] + jnp.dot(a_ref[...], w_ref[...], preferred_element_type=F32)


def _matmul_res(a, w, res, *, name):
    m, k = a.shape
    n = w.shape[1]
    tm, tn = _matmul_tiles(m, k, n, lambda tm, tn: 4 * tm * k + 4 * k * tn + 16 * tm * tn)
    return pl.pallas_call(
        _matmul_res_kernel,
        grid=(m // tm, n // tn),
        in_specs=[
            pl.BlockSpec((tm, k), lambda i, j: (i, 0)),
            pl.BlockSpec((k, tn), lambda i, j: (0, j)),
            pl.BlockSpec((tm, tn), lambda i, j: (i, j)),
        ],
        out_specs=pl.BlockSpec((tm, tn), lambda i, j: (i, j)),
        out_shape=jax.ShapeDtypeStruct((m, n), F32),
        compiler_params=_params("parallel", "parallel"),
        name=name,
    )(a, w, res)


def _s5_out_kernel(y_ref, wv_ref, wg_ref, r_ref, o_ref, h_ref):
    @pl.when(pl.program_id(1) == 0)
    def _():
        h_ref[...] = _gelu_tanh(y_ref[...].astype(F32)).astype(BF16)

    h = h_ref[...]
    val = jnp.dot(h, wv_ref[...], preferred_element_type=F32)
    gate = jnp.dot(h, wg_ref[...], preferred_element_type=F32)
    o_ref[...] = r_ref[...] + val * _sigmoid(gate)


def _s5_out(y, w_out, res):
    m, k = y.shape
    n = w_out.shape[1] // 2
    tm, tn = _tile(m, 2 * TM), _tile(n, TN // 2, LANES)
    nb = n // tn
    return pl.pallas_call(
        _s5_out_kernel,
        grid=(m // tm, nb),
        in_specs=[
            pl.BlockSpec((tm, k), lambda i, j: (i, 0)),
            pl.BlockSpec((k, tn), lambda i, j: (0, j)),
            pl.BlockSpec((k, tn), lambda i, j: (0, j + nb)),
            pl.BlockSpec((tm, tn), lambda i, j: (i, j)),
        ],
        out_specs=pl.BlockSpec((tm, tn), lambda i, j: (i, j)),
        out_shape=jax.ShapeDtypeStruct((m, n), F32),
        scratch_shapes=[pltpu.VMEM((tm, k), BF16)],
        compiler_params=_params("parallel", "arbitrary"),
        name="s5_out_glu",
    )(y, w_out, w_out, res)


def _swiglu_step(h, wg, wu, wd):
    g = jnp.dot(h, wg, preferred_element_type=F32)
    u = jnp.dot(h, wu, preferred_element_type=F32)
    a = (g * _sigmoid(g) * u).astype(BF16)
    return jnp.dot(a, wd, preferred_element_type=F32)


def _ffn_kernel(x_ref, g_ref, wg_ref, wu_ref, wd_ref, o_ref, h_ref, acc_ref):
    f = pl.program_id(1)

    @pl.when(f == 0)
    def _():
        h_ref[...] = _rms(x_ref[...], g_ref[...]).astype(BF16)
        acc_ref[...] = jnp.zeros_like(acc_ref)

    acc_ref[...] += _swiglu_step(h_ref[...], wg_ref[...], wu_ref[...], wd_ref[...])

    @pl.when(f == pl.num_programs(1) - 1)
    def _():
        o_ref[...] = x_ref[...] + acc_ref[...]


def _ffn(x, gain, w_gu, w_down, layer):
    m, d = x.shape
    ff = w_down.shape[1]
    tm, tf = _tile(m, TM), _tile(ff, TF, LANES)
    nf = ff // tf
    return pl.pallas_call(
        _ffn_kernel,
        grid=(m // tm, nf),
        in_specs=[
            pl.BlockSpec((tm, d), lambda i, f: (i, 0)),
            pl.BlockSpec((1, d), lambda i, f: (0, 0)),
            pl.BlockSpec((None, d, tf), lambda i, f: (layer, 0, f)),
            pl.BlockSpec((None, d, tf), lambda i, f: (layer, 0, f + nf)),
            pl.BlockSpec((None, tf, d), lambda i, f: (layer, f, 0)),
        ],
        out_specs=pl.BlockSpec((tm, d), lambda i, f: (i, 0)),
        out_shape=jax.ShapeDtypeStruct((m, d), F32),
        scratch_shapes=[pltpu.VMEM((tm, d), BF16), pltpu.VMEM((tm, d), F32)],
        compiler_params=_params("parallel", "arbitrary"),
        name="ffn_swiglu",
    )(x, gain.reshape(1, d).astype(F32), w_gu, w_gu, w_down)


def _rope_table(seq):
    half = MLA_ROPE // 2
    inv = ROPE_THETA ** (-jnp.arange(half, dtype=F32) / half)
    ang = jnp.arange(seq, dtype=F32)[:, None] * inv
    cos, sin = jnp.cos(ang), jnp.sin(ang)
    return jnp.concatenate([cos, cos, sin, sin], axis=-1)


def _rot_cols(w):
    half = w.shape[-1] // 2
    return jnp.concatenate([-w[..., half:], w[..., :half]], axis=-1)


def _rope_slab(slab, cs):
    prod = slab * cs
    return prod + pltpu.roll(prod, MLA_ROPE, axis=1)


def _krope_kernel(p_ref, cs_ref, o_ref):
    r = _rope_slab(p_ref[...].astype(F32), cs_ref[...])
    lane = lax.broadcasted_iota(jnp.int32, r.shape, 1)
    o_ref[...] = jnp.where(lane < MLA_ROPE, r, 0.0).astype(o_ref.dtype)


def _krope(proj, cs, seq, col_block):
    t = proj.shape[0]
    tm = _tile(seq, TM)
    ns = seq // tm
    return pl.pallas_call(
        _krope_kernel,
        grid=(t // tm,),
        in_specs=[
            pl.BlockSpec((tm, LANES), lambda i: (i, col_block)),
            pl.BlockSpec((tm, LANES), lambda i: (i % ns, 0)),
        ],
        out_specs=pl.BlockSpec((tm, LANES), lambda i: (i, 0)),
        out_shape=jax.ShapeDtypeStruct((t, LANES), BF16),
        compiler_params=_params("parallel"),
        name="mla_k_rope",
    )(proj, cs)


def _attn_kernel(qi_ref, ki_ref, q_ref, cs_ref, kv_ref, kr_ref, o_ref, qc_ref, m_ref, acc_ref):
    n_heads = qc_ref.shape[0]
    s_id = pl.program_id(2)
    qi = qi_ref[s_id]
    ki = ki_ref[s_id]
    hw = 2 * LANES

    @pl.when(ki == 0)
    def _():
        for hh in range(n_heads):
            q = q_ref[:, hh * hw:(hh + 1) * hw]
            qr = _rope_slab(q[:, LANES:].astype(F32), cs_ref[...]).astype(BF16)
            qc_ref[hh] = jnp.concatenate([q[:, :LANES], qr], axis=1)
        m_ref[...] = jnp.full_like(m_ref, -jnp.inf)
        acc_ref[...] = jnp.zeros_like(acc_ref)

    lane = lax.broadcasted_iota(jnp.int32, kr_ref.shape, 1)
    ones_col = jnp.where(lane == 0, 1.0, 0.0).astype(BF16)
    rq = q_ref.shape[0] // ATT_GROUPS

    def sweep(masked):
        for hh in range(n_heads):
            kc = jnp.concatenate([kv_ref[:, hh * hw:hh * hw + LANES], kr_ref[...]], axis=1)
            ve = jnp.concatenate([kv_ref[:, hh * hw + LANES:(hh + 1) * hw], ones_col], axis=1)

            def scores(g):
                return lax.dot_general(qc_ref[hh, g * rq:(g + 1) * rq, :], kc, (((1,), (1,)), ((), ())),
                                       preferred_element_type=F32)

            def update(g, sc):
                rows = slice(g * rq, (g + 1) * rq)
                if masked:
                    row = lax.broadcasted_iota(jnp.int32, sc.shape, 0) + g * rq
                    col = lax.broadcasted_iota(jnp.int32, sc.shape, 1)
                    sc = jnp.where(col <= row, sc, -jnp.inf)
                m_prev = m_ref[hh, rows, :]
                m_new = jnp.maximum(m_prev, jnp.max(sc, axis=-1, keepdims=True))
                alpha = jnp.exp2(m_prev - m_new)
                p = jnp.exp2(sc - m_new)
                acc_ref[hh, rows, :] = alpha * acc_ref[hh, rows, :] + jnp.dot(p.astype(BF16), ve,
                                                                              preferred_element_type=F32)
                m_ref[hh, rows, :] = m_new

            nxt = scores(0)
            for g in range(ATT_GROUPS):
                cur = nxt
                if g + 1 < ATT_GROUPS:
                    nxt = scores(g + 1)
                update(g, cur)

    @pl.when(ki < qi)
    def _():
        sweep(False)

    @pl.when(ki == qi)
    def _():
        sweep(True)
        for hh in range(n_heads):
            a = acc_ref[hh]
            o_ref[:, hh * LANES:(hh + 1) * LANES] = (a[:, :LANES] / a[:, LANES:LANES + 1]).astype(o_ref.dtype)


def _attention(q_ext, kv_ext, k_rope, cs, batch, seq):
    t = q_ext.shape[0]
    heads = q_ext.shape[1] // (2 * LANES)
    hps = math.gcd(heads, ATT_HEADS)
    hw = hps * 2 * LANES
    tq = _tile(seq, ATT_TILE)
    nq = seq // tq
    steps = [(i, j) for i in range(nq) for j in range(i + 1)]
    qi_tab = jnp.asarray([s[0] for s in steps], jnp.int32)
    ki_tab = jnp.asarray([s[1] for s in steps], jnp.int32)
    grid_spec = pltpu.PrefetchScalarGridSpec(
        num_scalar_prefetch=2,
        grid=(batch, heads // hps, len(steps)),
        in_specs=[
            pl.BlockSpec((tq, hw), lambda b, h, s, qi, ki: (b * nq + qi[s], h)),
            pl.BlockSpec((tq, LANES), lambda b, h, s, qi, ki: (qi[s], 0)),
            pl.BlockSpec((tq, hw), lambda b, h, s, qi, ki: (b * nq + ki[s], h)),
            pl.BlockSpec((tq, LANES), lambda b, h, s, qi, ki: (b * nq + ki[s], 0)),
        ],
        out_specs=pl.BlockSpec((tq, hps * LANES), lambda b, h, s, qi, ki: (b * nq + qi[s], h)),
        scratch_shapes=[
            pltpu.VMEM((hps, tq, 2 * LANES), BF16),
            pltpu.VMEM((hps, tq, 1), F32),
            pltpu.VMEM((hps, tq, 2 * LANES), F32),
        ],
    )
    return pl.pallas_call(
        _attn_kernel,
        grid_spec=grid_spec,
        out_shape=jax.ShapeDtypeStruct((t, heads * LANES), BF16),
        compiler_params=_params("parallel", "parallel", "arbitrary"),
        name="mla_flash_attention",
    )(qi_tab, ki_tab, q_ext, cs, kv_ext, k_rope)


def _mla(x, ln, w_in, q_norm, kv_norm, w_uq, w_ukv, w_o, batch, seq):
    heads = MLA_HEADS
    assert MLA_NOPE == LANES and MLA_V == LANES and 2 * MLA_ROPE == LANES
    assert MLA_Q_RANK == MLA_KV_RANK
    rank = MLA_Q_RANK
    w_kr = w_in[:, 2 * rank:]
    w_in_ext = jnp.concatenate([w_in, _rot_cols(w_kr)], axis=1).astype(BF16)
    proj = _pro_matmul(x, ln, w_in_ext, out_dtype=F32, name="mla_in_proj")
    scale = (MLA_NOPE + MLA_ROPE) ** -0.5 * math.log2(math.e)
    wq = w_uq.reshape(rank, heads, MLA_NOPE + MLA_ROPE) * scale
    wq_ext = jnp.concatenate([wq, _rot_cols(wq[..., MLA_NOPE:])], axis=-1)
    wq_ext = wq_ext.reshape(rank, heads * 2 * LANES).astype(BF16)
    q_ext = _pro_matmul(proj, q_norm, wq_ext, out_dtype=BF16, x_col_block=0, name="mla_q_up")
    kv_ext = _pro_matmul(proj, kv_norm, w_ukv.astype(BF16), out_dtype=BF16, x_col_block=1,
                         name="mla_kv_up")
    cs = _rope_table(seq)
    k_rope = _krope(proj, cs, seq, col_block=2 * rank // LANES)
    o = _attention(q_ext, kv_ext, k_rope, cs, batch, seq)
    return _matmul_res(o, w_o.astype(BF16), x, name="mla_out_proj")


def _hg_tables(c):
    levels = int(math.log2(c))
    assert 2 ** levels == c
    t = np.arange(c)[:, None]
    u = np.arange(c)[None, :]
    mats = []
    for l in range(1, levels + 1):
        mid = ((t >> l) << l) + (1 << (l - 1))
        upper = t >= mid
        mats.append(np.where(upper, (u >= mid) & (u <= t), (u > t) & (u <= mid - 1)))
    mats.append(u <= t)
    mats.append(u > t)
    masks = [t == u]
    for l in range(1, levels + 1):
        masks.append(((t >> l) == (u >> l)) & (((t >> (l - 1)) & 1) == 1) & (((u >> (l - 1)) & 1) == 0))
    tab = np.concatenate(mats, axis=0).astype(np.float32)
    return np.concatenate([tab] * 3, axis=1), np.concatenate(masks, axis=0).astype(np.float32), levels


def _split3(x):
    hi = x.astype(BF16)
    r = x - hi.astype(F32)
    mid = r.astype(BF16)
    lo = (r - mid.astype(F32)).astype(BF16)
    return hi, mid, lo


def _hgrn2_kernel(q_ref, z_ref, v_ref, g_ref, lb_ref, gn_ref, tab_ref, msk_ref, o_ref, st_ref, *, chunk, levels):
    c = chunk
    n_chunks = q_ref.shape[0] // c
    dk = q_ref.shape[1]

    @pl.when(pl.program_id(2) == 0)
    def _():
        st_ref[...] = jnp.zeros_like(st_ref)

    lb = lb_ref[...]
    log_lb = jnp.log(lb)
    log_1mlb = jnp.log1p(-lb)
    z = z_ref[...]
    q = q_ref[...].astype(F32)
    e = jnp.exp(-jnp.abs(z))
    log_sig = jnp.minimum(z, 0.0) - jnp.log1p(e)
    bterm = log_1mlb + log_sig
    lf = jnp.maximum(log_lb, bterm) + jnp.log1p(jnp.exp(-jnp.abs(log_lb - bterm)))
    k = (1.0 - lb) * jnp.where(z >= 0, e, 1.0) / (1.0 + e)

    def chunks_on_lanes(x):
        return jnp.concatenate([x[i * c:(i + 1) * c] for i in range(n_chunks)], axis=1)

    pieces = jnp.concatenate([chunks_on_lanes(p) for p in _split3(lf)], axis=0)
    sums = jnp.dot(tab_ref[...], pieces, preferred_element_type=F32)
    dec = jnp.exp(sums)
    diag = jnp.sum(q * k, axis=-1, keepdims=True)
    o_intra, q_in, upds, d_last = [], [], [], []
    for i in range(n_chunks):
        rows = slice(i * c, (i + 1) * c)
        qi, ki, vi = q[rows], k[rows], v_ref[rows, :]
        di = dec[:, i * dk:(i + 1) * dk]
        scores = msk_ref[0:c, :] * diag[rows]
        for l in range(1, levels + 1):
            d = di[(l - 1) * c:l * c]
            sl_ = lax.dot_general((qi * d).astype(BF16), (ki * d).astype(BF16), (((1,), (1,)), ((), ())),
                                  preferred_element_type=F32)
            scores = scores + msk_ref[l * c:(l + 1) * c, :] * sl_
        d_pre = di[levels * c:(levels + 1) * c]
        d_suf = di[(levels + 1) * c:(levels + 2) * c]
        o_intra.append(jnp.dot(scores.astype(BF16), vi, preferred_element_type=F32))
        q_in.append((qi * d_pre).astype(BF16))
        upds.append(lax.dot_general(vi, (ki * d_suf).astype(BF16), (((0,), (0,)), ((), ())),
                                    preferred_element_type=F32))
        d_last.append(d_pre[c - 1:c, :])
    st = st_ref[...]
    outs = []
    for i in range(n_chunks):
        outs.append(o_intra[i] + lax.dot_general(q_in[i], st.astype(BF16), (((1,), (1,)), ((), ())),
                                                 preferred_element_type=F32))
        st = st * d_last[i] + upds[i]
    st_ref[...] = st
    o = jnp.concatenate(outs, axis=0)
    gt = g_ref[...].astype(F32)
    o_ref[...] = (_rms(o, gn_ref[...]) * (gt * _sigmoid(gt))).astype(o_ref.dtype)


def _hgrn2_core(qvg, z, lb, g_norm, batch, seq):
    t = z.shape[0]
    heads = z.shape[1] // LANES
    ts = _tile(seq, HG_TILE)
    ns = seq // ts
    c = min(HG_CHUNK, ts)
    tab, msk, levels = _hg_tables(c)
    tab = jnp.asarray(tab, BF16)
    msk = jnp.asarray(msk, F32)
    return pl.pallas_call(
        functools.partial(_hgrn2_kernel, chunk=c, levels=levels),
        grid=(batch, heads, ns),
        in_specs=[
            pl.BlockSpec((ts, LANES), lambda b, h, s: (b * ns + s, h)),
            pl.BlockSpec((ts, LANES), lambda b, h, s: (b * ns + s, h)),
            pl.BlockSpec((ts, LANES), lambda b, h, s: (b * ns + s, heads + h)),
            pl.BlockSpec((ts, LANES), lambda b, h, s: (b * ns + s, 2 * heads + h)),
            pl.BlockSpec((1, LANES), lambda b, h, s: (0, h)),
            pl.BlockSpec((1, LANES), lambda b, h, s: (0, 0)),
            pl.BlockSpec(tab.shape, lambda b, h, s: (0, 0)),
            pl.BlockSpec(msk.shape, lambda b, h, s: (0, 0)),
        ],
        out_specs=pl.BlockSpec((ts, LANES), lambda b, h, s: (b * ns + s, h)),
        out_shape=jax.ShapeDtypeStruct((t, heads * LANES), BF16),
        scratch_shapes=[pltpu.VMEM((LANES, LANES), F32)],
        compiler_params=_params("parallel", "parallel", "arbitrary"),
        name="hgrn2_recurrence",
    )(qvg, z, qvg, qvg, lb.reshape(1, -1).astype(F32), g_norm.reshape(1, -1).astype(F32), tab, msk)


def _hgrn2(x, ln, w_in, lb, g_norm, w_o, batch, seq):
    d = x.shape[1]
    assert d == HG_HEADS * LANES
    w_qvg = jnp.concatenate([w_in[:, :d], w_in[:, 2 * d:]], axis=1).astype(BF16)
    w_z = w_in[:, d:2 * d].astype(BF16)
    qvg = _pro_matmul(x, ln, w_qvg, out_dtype=BF16, name="hg_in_proj_qvg")
    z = _pro_matmul(x, ln, w_z, out_dtype=F32, name="hg_in_proj_z")
    o = _hgrn2_core(qvg, z, lb, g_norm, batch, seq)
    return _matmul_res(o, w_o.astype(BF16), x, name="hg_out_proj")


def _s5_tables(a_re, a_im, b_re, b_im, c_re, c_im, d_skip, log_dt, nblk):
    L = S5_BLOCK
    G, P = a_re.shape
    N = b_re.shape[-1]
    a = lax.complex(a_re.astype(F32), a_im.astype(F32))
    dt_a = a * jnp.exp(log_dt.astype(F32))[:, None]
    a_bar = jnp.exp(dt_a)
    b_bar = ((a_bar - 1.0) / a)[:, :, None] * lax.complex(b_re.astype(F32), b_im.astype(F32))
    cc = lax.complex(c_re.astype(F32), c_im.astype(F32))
    pw = jnp.exp(jnp.arange(L + 1, dtype=F32)[:, None, None] * dt_a)
    kern = jnp.real(jnp.einsum("gnp,tgp,gpm->gtnm", cc, pw[:L], b_bar))
    kern = kern.at[:, 0].add(jax.vmap(jnp.diag)(d_skip.astype(F32)))
    s_idx = jnp.arange(L)[:, None]
    t_idx = jnp.arange(L)[None, :]
    lag = t_idx - s_idx
    toep = jnp.where((lag >= 0)[None, :, :, None, None], kern[:, jnp.clip(lag, 0, L - 1)], 0.0)
    toep = toep.transpose(0, 1, 4, 2, 3).reshape(G, L * N, L * N)
    wst = jnp.einsum("sgp,gpm->gsmp", pw[:L][::-1], b_bar)
    w_st = jnp.concatenate([jnp.real(wst), jnp.imag(wst)], axis=-1).reshape(G, L * N, 2 * P)
    co = jnp.einsum("gnp,tgp->gptn", cc, pw[1:])
    c_out = jnp.concatenate([jnp.real(co), -jnp.imag(co)], axis=1).reshape(G, 2 * P, L * N)
    nsteps = max(1, int(math.ceil(math.log2(nblk))))
    lam_k = jnp.exp((L * 2.0 ** jnp.arange(nsteps, dtype=F32))[None, :, None] * dt_a[:, None, :])
    lr, li = jnp.real(lam_k), jnp.imag(lam_k)
    lam = jnp.stack([jnp.concatenate([lr, lr], -1), jnp.concatenate([-li, li], -1)], axis=2)
    return toep.astype(BF16), w_st.astype(BF16), c_out.astype(BF16), lam.astype(F32)


def _s5_perm():
    gpl, N = LANES // S5_GROUP, S5_GROUP
    a = np.arange(gpl * LANES)
    tau, g, n = a // LANES, (a % LANES) // N, a % N
    p = np.zeros((gpl * LANES, gpl * LANES), np.float32)
    p[a, g * LANES + tau * N + n] = 1.0
    return p


def _s5_kernel(u_ref, perm_ref, toep_ref, wst_ref, cout_ref, lam_ref, y_ref, *, nsteps):
    L, N = S5_BLOCK, S5_GROUP
    nblk = u_ref.shape[0] // L
    gpl = LANES // N
    nchunk = L // gpl
    perm = perm_ref[...]
    v = [u_ref[pl.ds(t, nblk, stride=L), :].astype(BF16) for t in range(L)]
    uc = [jnp.dot(jnp.concatenate(v[j * gpl:(j + 1) * gpl], axis=1), perm,
                  preferred_element_type=F32).astype(BF16) for j in range(nchunk)]
    blk = lax.broadcasted_iota(jnp.int32, (nblk, wst_ref.shape[-1]), 0)
    ys = []
    for g in range(gpl):
        ug = jnp.concatenate([uc[j][:, g * LANES:(g + 1) * LANES] for j in range(nchunk)], axis=1)
        z = jnp.dot(ug, wst_ref[g], preferred_element_type=F32)
        p2 = z.shape[1]
        h = z
        for k in range(nsteps):
            sh = 1 << k
            prev = jnp.where(blk >= sh, pltpu.roll(h, sh, axis=0), 0.0)
            lam = lam_ref[g, k]
            h = h + prev * lam[0:1, :] + pltpu.roll(prev, p2 // 2, axis=1) * lam[1:2, :]
        h0 = jnp.where(blk >= 1, pltpu.roll(h, 1, axis=0), 0.0)
        yg = (jnp.dot(ug, toep_ref[g], preferred_element_type=F32)
              + jnp.dot(h0.astype(BF16), cout_ref[g], preferred_element_type=F32))
        ys.append(yg.astype(BF16))
    for j in range(nchunk):
        yj = jnp.concatenate([ys[g][:, j * LANES:(j + 1) * LANES] for g in range(gpl)], axis=1)
        wj = jnp.dot(yj, perm, preferred_element_type=F32)
        for tau in range(gpl):
            y_ref[pl.ds(j * gpl + tau, nblk, stride=L), :] = wj[:, tau * LANES:(tau + 1) * LANES]


def _s5_core(u, toep, w_st, c_out, lam, batch, seq):
    t, d = u.shape
    gpl = LANES // S5_GROUP
    ln = toep.shape[-1]
    p2 = w_st.shape[-1]
    nsteps = lam.shape[1]
    perm = jnp.asarray(_s5_perm(), BF16)
    return pl.pallas_call(
        functools.partial(_s5_kernel, nsteps=nsteps),
        grid=(d // LANES, batch),
        in_specs=[
            pl.BlockSpec((seq, LANES), lambda i, b: (b, i)),
            pl.BlockSpec(perm.shape, lambda i, b: (0, 0)),
            pl.BlockSpec((gpl, ln, ln), lambda i, b: (i, 0, 0)),
            pl.BlockSpec((gpl, ln, p2), lambda i, b: (i, 0, 0)),
            pl.BlockSpec((gpl, p2, ln), lambda i, b: (i, 0, 0)),
            pl.BlockSpec((gpl, nsteps, 2, p2), lambda i, b: (i, 0, 0, 0)),
        ],
        out_specs=pl.BlockSpec((seq, LANES), lambda i, b: (b, i)),
        out_shape=jax.ShapeDtypeStruct((t, d), F32),
        compiler_params=_params("parallel", "parallel"),
        name="s5_ssm",
    )(u, perm, toep, w_st, c_out, lam)


def _s5(x, ln, w_in, a_re, a_im, b_re, b_im, c_re, c_im, d_skip, log_dt, w_out, batch, seq):
    nblk = seq // S5_BLOCK
    u = _pro_matmul(x, ln, w_in.astype(BF16), out_dtype=F32, name="s5_in_proj")
    toep, w_st, c_out, lam = _s5_tables(a_re, a_im, b_re, b_im, c_re, c_im, d_skip, log_dt, nblk)
    y = _s5_core(u, toep, w_st, c_out, lam, batch, seq)
    return _s5_out(y, w_out.astype(BF16), x)


def _router_kernel(x_ref, g_ref, w_ref, h_ref, r_ref):
    h = _rms(x_ref[...], g_ref[...])
    h_ref[...] = h
    h0 = h.astype(BF16)
    h1 = (h - h0.astype(F32)).astype(BF16)
    w = w_ref[...]
    w0 = w.astype(BF16)
    w1 = (w - w0.astype(F32)).astype(BF16)
    logits = (jnp.dot(h0, w0, preferred_element_type=F32) + jnp.dot(h0, w1, preferred_element_type=F32)
              + jnp.dot(h1, w0, preferred_element_type=F32))
    lane = lax.broadcasted_iota(jnp.int32, logits.shape, 1).astype(F32)
    neg = -jnp.inf
    lg = jnp.where(lane < N_EXPERTS, logits, neg)
    m1 = jnp.max(lg, axis=-1, keepdims=True)
    i1 = jnp.min(jnp.where(lg == m1, lane, float(LANES)), axis=-1, keepdims=True)
    lg2 = jnp.where(lane == i1, neg, lg)
    m2 = jnp.max(lg2, axis=-1, keepdims=True)
    i2 = jnp.min(jnp.where(lg2 == m2, lane, float(LANES)), axis=-1, keepdims=True)
    e2 = jnp.exp(m2 - m1)
    g1 = 1.0 / (1.0 + e2)
    g2 = e2 / (1.0 + e2)
    out = jnp.where(lane == 0, g1, jnp.where(lane == 1, g2, jnp.where(lane == 2, i1, jnp.where(lane == 3, i2, 0.0))))
    r_ref[...] = out[:, :r_ref.shape[1]]


def _router(x, gain, w_router):
    m, d = x.shape
    tm = _tile(m, TM)
    w_pad = jnp.zeros((d, LANES), F32).at[:, :N_EXPERTS].set(w_router.astype(F32))
    return pl.pallas_call(
        _router_kernel,
        grid=(m // tm,),
        in_specs=[
            pl.BlockSpec((tm, d), lambda i: (i, 0)),
            pl.BlockSpec((1, d), lambda i: (0, 0)),
            pl.BlockSpec((d, LANES), lambda i: (0, 0)),
        ],
        out_specs=[pl.BlockSpec((tm, d), lambda i: (i, 0)), pl.BlockSpec((tm, 8), lambda i: (i, 0))],
        out_shape=[jax.ShapeDtypeStruct((m, d), F32), jax.ShapeDtypeStruct((m, 8), F32)],
        compiler_params=_params("parallel"),
        name="moe_router_top2",
    )(x, gain.reshape(1, d).astype(F32), w_pad)


def _moe_kernel(te_ref, ta_ref, tv_ref, idx_ref, h_hbm, wg_ref, wu_ref, wd_ref, y_hbm,
                x_ref, xb_ref, acc_ref, yb_ref, pend_ref, sem_g, sem_s):
    i = pl.program_id(0)
    f = pl.program_id(1)
    nt = pl.num_programs(0)
    last_f = pl.num_programs(1) - 1
    tm = xb_ref.shape[0]
    slot = lax.rem(i, 2)
    active = ta_ref[i] == 1
    nxt = jnp.minimum(i + 1, nt - 1)
    carry = active & (i + 1 < nt) & (ta_ref[nxt] == 1)
    n_dma_steps = tm // GATHER_CHUNK

    def start_gather(tile, sl):
        def body(o, c):
            base = pl.multiple_of(o * DMA_UNROLL, DMA_UNROLL)
            for j in range(DMA_UNROLL):
                r = base + j
                pltpu.make_async_copy(h_hbm.at[pl.ds(idx_ref[tile, r], 1)], x_ref.at[sl, pl.ds(r, 1)],
                                      sem_g.at[sl]).start()
            return c

        lax.fori_loop(0, tm // DMA_UNROLL, body, 0)

    def wait_gather(sl):
        pltpu.make_async_copy(h_hbm.at[pl.ds(0, tm)], x_ref.at[sl], sem_g.at[sl]).wait()

    def scatter_row(r):
        pltpu.make_async_copy(yb_ref.at[pl.ds(r, 1)], y_hbm.at[pl.ds(idx_ref[i, tm + r], 1)], sem_s).start()

    def start_scatter():
        @pl.when(tv_ref[i] == tm)
        def _():
            def body(o, c):
                base = pl.multiple_of(o * DMA_UNROLL, DMA_UNROLL)
                for j in range(DMA_UNROLL):
                    scatter_row(base + j)
                return c

            lax.fori_loop(0, tm // DMA_UNROLL, body, 0)

        @pl.when(tv_ref[i] < tm)
        def _():
            def body(r, c):
                scatter_row(r)
                return c

            lax.fori_loop(0, tv_ref[i], body, 0)

    def wait_scatter():
        @pl.when(pend_ref[0] > 0)
        def _():
            nv = tv_ref[jnp.maximum(pend_ref[0] - 1, 0)]
            for bit in reversed(range(tm.bit_length())):
                n = 1 << bit

                @pl.when((nv & n) != 0)
                def _():
                    pltpu.make_async_copy(yb_ref.at[pl.ds(0, n)], y_hbm.at[pl.ds(0, n)], sem_s).wait()

            pend_ref[0] = 0

    @pl.when((i == 0) & (f == 0))
    def _():
        pend_ref[0] = 0

        @pl.when(active)
        def _():
            start_gather(0, 0)

    @pl.when(active & (f == 0))
    def _():
        wait_gather(slot)
        xb_ref[...] = x_ref[slot].astype(BF16)
        acc_ref[...] = jnp.zeros_like(acc_ref)

    @pl.when(carry & (f < n_dma_steps))
    def _():
        base = pl.multiple_of(f * GATHER_CHUNK, GATHER_CHUNK)
        for j in range(GATHER_CHUNK):
            r = base + j
            pltpu.make_async_copy(h_hbm.at[pl.ds(idx_ref[nxt, r], 1)], x_ref.at[1 - slot, pl.ds(r, 1)],
                                  sem_g.at[1 - slot]).start()
        acc_ref[...] += _swiglu_step(xb_ref[...], wg_ref[...], wu_ref[...], wd_ref[...])

    @pl.when(active & jnp.logical_not(carry & (f < n_dma_steps)))
    def _():
        acc_ref[...] += _swiglu_step(xb_ref[...], wg_ref[...], wu_ref[...], wd_ref[...])

    @pl.when(active & (f == last_f))
    def _():
        wait_scatter()
        yb_ref[...] = acc_ref[...]
        start_scatter()
        pend_ref[0] = i + 1

    @pl.when((i == nt - 1) & (f == last_f))
    def _():
        wait_scatter()


def _moe_experts(h, idx, tile_expert, tile_active, tile_valid, w_gu, w_down, layer):
    n_tok, d = h.shape
    n_tiles, tm = idx.shape[0], idx.shape[1] // 2
    ff = w_down.shape[2]
    tf = _tile(ff, TF, LANES)
    nf = ff // tf
    assert tm % GATHER_CHUNK == 0 and tm // GATHER_CHUNK <= nf and tm % DMA_UNROLL == 0

    def fe(f, i, ta):
        return jnp.where(ta[i] == 1, f, nf - 1)

    grid_spec = pltpu.PrefetchScalarGridSpec(
        num_scalar_prefetch=4,
        grid=(n_tiles, nf),
        in_specs=[
            pl.BlockSpec(memory_space=pl.ANY),
            pl.BlockSpec((None, None, d, tf), lambda i, f, te, ta, tv, ix: (layer, te[i], 0, fe(f, i, ta))),
            pl.BlockSpec((None, None, d, tf), lambda i, f, te, ta, tv, ix: (layer, te[i], 0, fe(f, i, ta) + nf)),
            pl.BlockSpec((None, None, tf, d), lambda i, f, te, ta, tv, ix: (layer, te[i], fe(f, i, ta), 0)),
        ],
        out_specs=pl.BlockSpec(memory_space=pl.ANY),
        scratch_shapes=[
            pltpu.VMEM((2, tm, d), F32),
            pltpu.VMEM((tm, d), BF16),
            pltpu.VMEM((tm, d), F32),
            pltpu.VMEM((tm, d), F32),
            pltpu.SMEM((1,), jnp.int32),
            pltpu.SemaphoreType.DMA((2,)),
            pltpu.SemaphoreType.DMA,
        ],
    )
    return pl.pallas_call(
        _moe_kernel,
        grid_spec=grid_spec,
        out_shape=jax.ShapeDtypeStruct((TOP_K * n_tok, d), F32),
        compiler_params=_params("arbitrary", "arbitrary"),
        name="moe_grouped_swiglu",
    )(tile_expert, tile_active, tile_valid, idx, h, w_gu, w_gu, w_down)


def _combine_kernel(x_ref, a_ref, b_ref, r_ref, g_ref, o_ref, *, out_norm):
    r = r_ref[...]
    out = x_ref[...] + (r[:, 0:1] * a_ref[...] + r[:, 1:2] * b_ref[...])
    o_ref[...] = _rms(out, g_ref[...]) if out_norm else out


def _combine(x, y, route, out_gain=None):
    m, d = x.shape
    tm = _tile(m, TM)
    nb = m // tm
    gain = jnp.ones((d,), F32) if out_gain is None else out_gain
    return pl.pallas_call(
        functools.partial(_combine_kernel, out_norm=out_gain is not None),
        grid=(nb,),
        in_specs=[
            pl.BlockSpec((tm, d), lambda i: (i, 0)),
            pl.BlockSpec((tm, d), lambda i: (i, 0)),
            pl.BlockSpec((tm, d), lambda i: (i + nb, 0)),
            pl.BlockSpec((tm, route.shape[1]), lambda i: (i, 0)),
            pl.BlockSpec((1, d), lambda i: (0, 0)),
        ],
        out_specs=pl.BlockSpec((tm, d), lambda i: (i, 0)),
        out_shape=jax.ShapeDtypeStruct((m, d), F32),
        compiler_params=_params("parallel"),
        name="moe_combine",
    )(x, y, y, route, gain.reshape(1, d).astype(F32))


def _route_metadata(route, n_tok, tm):
    experts = route[:, TOP_K:2 * TOP_K].T.reshape(-1).astype(jnp.int32)
    n_pairs = TOP_K * n_tok
    onehot = (experts[:, None] == jnp.arange(N_EXPERTS)[None, :]).astype(jnp.int32)
    rank = jnp.take_along_axis(jnp.cumsum(onehot, axis=0), experts[:, None], axis=1)[:, 0] - 1
    counts = jnp.sum(onehot, axis=0)
    padded = ((counts + tm - 1) // tm) * tm
    ends = jnp.cumsum(padded)
    starts = ends - padded
    rows = starts[experts] + rank
    n_rows = n_pairs + N_EXPERTS * tm
    n_rows = (n_rows // tm) * tm
    dest = jnp.full((n_rows,), -1, jnp.int32).at[rows].set(jnp.arange(n_pairs, dtype=jnp.int32))
    src = jnp.where(dest < 0, 0, dest % n_tok)
    idx = jnp.concatenate([src.reshape(-1, tm), jnp.maximum(dest, 0).reshape(-1, tm)], axis=1)
    tile_start = jnp.arange(n_rows // tm, dtype=jnp.int32) * tm
    tile_expert = jnp.sum((tile_start[:, None] >= ends[None, :]).astype(jnp.int32), axis=1)
    tile_expert = jnp.minimum(tile_expert, N_EXPERTS - 1)
    tile_active = (tile_start < ends[-1]).astype(jnp.int32)
    tile_valid = jnp.clip((starts + counts)[tile_expert] - tile_start, 0, tm).astype(jnp.int32) * tile_active
    return idx, tile_expert, tile_active, tile_valid


def _moe(x, gain, w_router, w_gu, w_down, layer, out_gain=None):
    n_tok = x.shape[0]
    h, route = _router(x, gain, w_router)
    tm = min(MOE_TM, n_tok)
    idx, tile_expert, tile_active, tile_valid = _route_metadata(route, n_tok, tm)
    y = _moe_experts(h, idx, tile_expert, tile_active, tile_valid, w_gu, w_down, layer)
    return _combine(x, y, route, out_gain)


def _final_norm_kernel(x_ref, g_ref, o_ref):
    o_ref[...] = _rms(x_ref[...], g_ref[...])


def _final_norm(x, gain):
    m, d = x.shape
    tm = _tile(m, TM)
    return pl.pallas_call(
        _final_norm_kernel,
        grid=(m // tm,),
        in_specs=[pl.BlockSpec((tm, d), lambda i: (i, 0)), pl.BlockSpec((1, d), lambda i: (0, 0))],
        out_specs=pl.BlockSpec((tm, d), lambda i: (i, 0)),
        out_shape=jax.ShapeDtypeStruct((m, d), F32),
        compiler_params=_params("parallel"),
        name="final_rmsnorm",
    )(x, gain.reshape(1, d).astype(F32))


def kernel(x, ln_mix, ln_ffn, ln_final, mla_w_in, mla_q_norm, mla_kv_norm, mla_w_uq, mla_w_ukv, mla_w_o, hg_w_in, hg_lower_bound, hg_g_norm, hg_w_o, s5_w_in, s5_a_re, s5_a_im, s5_b_re, s5_b_im, s5_c_re, s5_c_im, s5_d, s5_log_dt, s5_w_out, ffn_w_gu, ffn_w_down, moe_w_router, moe_w_gu, moe_w_down):
    batch, seq, d = x.shape
    depth = ln_mix.shape[0]
    lb_w = jax.nn.softmax(hg_lower_bound.astype(F32), axis=0)
    lower_bounds = jnp.cumsum(lb_w, axis=0) - lb_w[0]
    xt = x.reshape(batch * seq, d).astype(F32)
    ffn_gu, ffn_down = ffn_w_gu.astype(BF16), ffn_w_down.astype(BF16)
    moe_gu, moe_down = moe_w_gu.astype(BF16), moe_w_down.astype(BF16)
    for i in range(depth):
        m, j = i % N_MIXERS, i // N_MIXERS
        if m == 0:
            xt = _mla(xt, ln_mix[i], mla_w_in[j], mla_q_norm[j], mla_kv_norm[j], mla_w_uq[j], mla_w_ukv[j],
                      mla_w_o[j], batch, seq)
        elif m == 1:
            xt = _hgrn2(xt, ln_mix[i], hg_w_in[j], lower_bounds[i], hg_g_norm[j], hg_w_o[j], batch, seq)
        else:
            xt = _s5(xt, ln_mix[i], s5_w_in[j], s5_a_re[j], s5_a_im[j], s5_b_re[j], s5_b_im[j], s5_c_re[j],
                     s5_c_im[j], s5_d[j], s5_log_dt[j], s5_w_out[j], batch, seq)
        f = i // 2
        if i % 2 == 0:
            xt = _ffn(xt, ln_ffn[i], ffn_gu, ffn_down, f)
        else:
            xt = _moe(xt, ln_ffn[i], moe_w_router[f], moe_gu, moe_down, f,
                      out_gain=ln_final if i == depth - 1 else None)
    if depth % 2 == 1 or depth == 0:
        xt = _final_norm(xt, ln_final)
    return xt.reshape(batch, seq, d)
```
